```python
import jax, jax.numpy as jnp
from jax import lax
import numpy as np

D_MODEL = 1024
BATCH = 8
SEQ = 2048
DEPTH = 2
DEC_BATCH = 128
DEC_SEQ = 1
PAST_LEN = 16384
PAGE_SIZE = 128

HEAD_SIZE = 64
N_HEADS = D_MODEL // HEAD_SIZE
D_RWKV = N_HEADS * HEAD_SIZE
D_DECAY_LORA = 64
D_AAA_LORA = 64
D_GATE_LORA = 128
POOL_WINDOWS = (2, 4, 8, 16)
N_POOL_GROUPS = len(POOL_WINDOWS)
POOL_GROUP = D_MODEL // 8
D_POOL = N_POOL_GROUPS * POOL_GROUP
POOL_BUF = max(POOL_WINDOWS) - 1
D_FF = (D_MODEL * 11) // 4
CONV_W = 3
D_SHIFT = 3 * D_RWKV + D_DECAY_LORA + D_AAA_LORA + D_GATE_LORA
IN_COLS = D_SHIFT + D_POOL + 2 * D_MODEL
ALPHA = float((2 * DEPTH) ** 0.25)
BETA = float((8 * DEPTH) ** -0.25)
LN_EPS = 1e-5
GN_EPS = 64e-5

kernel_name = 'hybrid_pool_rwkv7_convffn_deepnorm_step'


def layer_norm(x, w=None, b=None):
    xf = x.astype(jnp.float32)
    mu = jnp.mean(xf, axis=-1, keepdims=True)
    var = jnp.mean(jnp.square(xf - mu), axis=-1, keepdims=True)
    y = (xf - mu) * lax.rsqrt(var + LN_EPS)
    if w is not None:
        y = y * w + b
    return y


def rwkv7_recurrence(state0, r, w, k, v, kk, a):
    def step(S, inp):
        r_t, w_t, k_t, v_t, kk_t, a_t = inp
        sa = jnp.einsum('bhvk,bhk->bhv', S, kk_t)
        S = (S * w_t[:, :, None, :]
             - sa[..., None] * (kk_t * a_t)[:, :, None, :]
             + v_t[..., None] * k_t[:, :, None, :])
        y = jnp.einsum('bhvk,bhk->bhv', S, r_t)
        return S, y
    xs = (jnp.moveaxis(r, 1, 0), jnp.moveaxis(w, 1, 0), jnp.moveaxis(k, 1, 0),
          jnp.moveaxis(v, 1, 0), jnp.moveaxis(kk, 1, 0), jnp.moveaxis(a, 1, 0))
    S, ys = lax.scan(step, state0, xs)
    return jnp.moveaxis(ys, 0, 1), S


def pool_mix(p_in, buf, pos0, pool_w, pool_scale):
    B, T, _ = p_in.shape
    ext = jnp.concatenate([buf.astype(jnp.float32), p_in.astype(jnp.float32)], axis=1)
    cs = jnp.pad(jnp.cumsum(ext, axis=1), ((0, 0), (1, 0), (0, 0)))
    end = cs[:, POOL_BUF + 1:]
    pos = pos0 + jnp.arange(T)
    groups = []
    for g, win in enumerate(POOL_WINDOWS):
        sl = slice(g * POOL_GROUP, (g + 1) * POOL_GROUP)
        start = cs[:, POOL_BUF + 1 - win: POOL_BUF + 1 - win + T, sl]
        cnt = jnp.minimum(win, pos + 1).astype(jnp.float32)[None, :, None]
        groups.append((end[..., sl] - start) / cnt)
    pooled = jnp.stack(groups, axis=2)
    diff = pooled - p_in.astype(jnp.float32).reshape(B, T, N_POOL_GROUPS, POOL_GROUP)
    mixed = jnp.einsum('btgc,gcd->btgd', diff, pool_w).reshape(B, T, D_POOL) * pool_scale
    return mixed, ext[:, -POOL_BUF:]


def token_mixer(h, shift_prev, wkv_prev, pool_prev, pos0, l, p):
    B, T, _ = h.shape
    w_in = p['w_in'][l]
    proj = jnp.einsum('btd,de->bte', h, w_in)
    prev_proj = jnp.einsum('bd,de->be', shift_prev.astype(jnp.float32), w_in[:, :D_SHIFT])
    cur = proj[..., :D_SHIFT]
    prev_rows = jnp.concatenate([prev_proj[:, None], cur[:, :-1]], axis=1)
    xs = cur + (prev_rows - cur) * p['mu_shift'][l]
    splits = [D_RWKV, 2 * D_RWKV, 3 * D_RWKV, 3 * D_RWKV + D_DECAY_LORA,
              3 * D_RWKV + D_DECAY_LORA + D_AAA_LORA]
    r, k, v, wl, al, gl = jnp.split(xs, splits, axis=-1)
    w_log = -jax.nn.softplus(-(p['decay_w0'][l] + jnp.tanh(wl) @ p['decay_up'][l])) - 0.5
    decay = jnp.exp(-jnp.exp(w_log))
    a = jax.nn.sigmoid(p['aaa_a0'][l] + al @ p['aaa_up'][l])
    g = jax.nn.sigmoid(gl) @ p['gate_up'][l]
    kk = k * p['k_k'][l]
    k = k * (1.0 + (a - 1.0) * p['k_a'][l])
    hs = lambda t: t.reshape(B, T, N_HEADS, HEAD_SIZE)
    kk_h = hs(kk)
    kk_h = kk_h / jnp.maximum(jnp.sqrt(jnp.sum(jnp.square(kk_h), axis=-1, keepdims=True)), 1e-12)
    r_h, k_h, v_h = hs(r), hs(k), hs(v)
    y, S_new = rwkv7_recurrence(wkv_prev.astype(jnp.float32), r_h, hs(decay), k_h, v_h, kk_h, hs(a))
    mu = jnp.mean(y, axis=-1, keepdims=True)
    var = jnp.mean(jnp.square(y - mu), axis=-1, keepdims=True)
    y = ((y - mu) * lax.rsqrt(var + GN_EPS)).reshape(B, T, D_RWKV) * p['gn_w'][l] + p['gn_b'][l]
    bonus = jnp.sum(r_h * k_h * p['r_k'][l], axis=-1, keepdims=True) * v_h
    y = y + bonus.reshape(B, T, D_RWKV)
    y_rwkv = (y * g) @ p['w_branch_rwkv'][l]
    p_in = proj[..., D_SHIFT:D_SHIFT + D_POOL]
    pooled, pool_new = pool_mix(p_in, pool_prev, pos0, p['pool_w'][l], p['pool_scale'][l])
    y_pool = pooled @ p['w_branch_pool'][l]
    gate_pool, gate_rwkv = jnp.split(proj[..., D_SHIFT + D_POOL:], 2, axis=-1)
    merged = jax.nn.sigmoid(gate_pool) * y_pool + jax.nn.sigmoid(gate_rwkv) * y_rwkv
    out = merged @ p['w_out'][l]
    return out, h[:, -1], S_new, pool_new


def conv_ffn(h, conv_prev, l, p):
    T = h.shape[1]
    u = jnp.einsum('btd,df->btf', h, p['ffn_up'][l])
    ext = jnp.concatenate([conv_prev.astype(jnp.float32), u], axis=1)
    cw = p['conv_w'][l]
    conv = p['conv_b'][l] + ext[:, 0:T] * cw[0] + ext[:, 1:T + 1] * cw[1] + ext[:, 2:T + 2] * cw[2]
    val, gate = jnp.split(conv, 2, axis=-1)
    out = (val * jax.nn.silu(gate)) @ p['ffn_down'][l]
    return out, ext[:, -(CONV_W - 1):]


def run_trunk(x, c, pos0, shift_st, wkv_st, pool_st, conv_st, p):
    x = x.astype(jnp.float32)
    c = c.astype(jnp.float32)
    n_shift, n_wkv, n_pool, n_conv = [], [], [], []
    for l in range(DEPTH):
        mod = jnp.einsum('bd,de->be', c, p['ada_w'][l]) + p['ada_b'][l]
        sh1, sc1, g1, sh2, sc2, g2 = jnp.split(mod[:, None, :], 6, axis=-1)
        h = layer_norm(x) * (1.0 + sc1) + sh1
        mix, s_shift, s_wkv, s_pool = token_mixer(h, shift_st[l], wkv_st[l], pool_st[l], pos0, l, p)
        x = layer_norm(ALPHA * x + (1.0 + g1) * mix, p['ln1_w'][l], p['ln1_b'][l])
        h2 = layer_norm(x) * (1.0 + sc2) + sh2
        ff, s_conv = conv_ffn(h2, conv_st[l], l, p)
        x = layer_norm(ALPHA * x + (1.0 + g2) * ff, p['ln2_w'][l], p['ln2_b'][l])
        n_shift.append(s_shift)
        n_wkv.append(s_wkv)
        n_pool.append(s_pool)
        n_conv.append(s_conv)
    return x, jnp.stack(n_shift), jnp.stack(n_wkv), jnp.stack(n_pool), jnp.stack(n_conv)


def setup_inputs(seed: int = 0) -> dict:
    key = jax.random.key(seed)
    ks = iter(jax.random.split(key, 48))
    nrm = lambda shape, s: jax.random.normal(next(ks), shape, jnp.float32) * s
    L, D = DEPTH, D_MODEL
    return {
        'x_prompt': nrm((BATCH, SEQ, D), 1.0),
        'x_sample': nrm((DEC_BATCH, DEC_SEQ, D), 1.0),
        'state_shift': nrm((L, DEC_BATCH, D), 1.0),
        'state_wkv': nrm((L, DEC_BATCH, N_HEADS, HEAD_SIZE, HEAD_SIZE), 0.3),
        'state_pool': nrm((L, DEC_BATCH, POOL_BUF, D_POOL), 1.0),
        'state_conv': nrm((L, DEC_BATCH, CONV_W - 1, 2 * D_FF), 1.0),
        'c_prompt': nrm((BATCH, D), 1.0),
        'c_sample': nrm((DEC_BATCH, D), 1.0),
        'ada_w': nrm((L, D, 6 * D), 0.2 * D ** -0.5),
        'ada_b': nrm((L, 6 * D), 0.02),
        'w_in': nrm((L, D, IN_COLS), D ** -0.5),
        'mu_shift': jax.random.uniform(next(ks), (L, D_SHIFT), jnp.float32),
        'decay_w0': jnp.linspace(-6.0, 1.0, D_RWKV, dtype=jnp.float32)[None, :] + nrm((L, D_RWKV), 0.1),
        'decay_up': nrm((L, D_DECAY_LORA, D_RWKV), 0.1 * D_DECAY_LORA ** -0.5),
        'aaa_a0': nrm((L, D_RWKV), 0.1),
        'aaa_up': nrm((L, D_AAA_LORA, D_RWKV), 0.5 * D_AAA_LORA ** -0.5),
        'gate_up': nrm((L, D_GATE_LORA, D_RWKV), D_GATE_LORA ** -0.5),
        'k_k': 0.85 + nrm((L, D_RWKV), 0.05),
        'k_a': 1.0 + nrm((L, D_RWKV), 0.05),
        'r_k': nrm((L, N_HEADS, HEAD_SIZE), 0.1),
        'gn_w': 1.0 + nrm((L, D_RWKV), 0.05),
        'gn_b': nrm((L, D_RWKV), 0.01),
        'w_branch_rwkv': nrm((L, D_RWKV, D), D_RWKV ** -0.5),
        'pool_w': nrm((L, N_POOL_GROUPS, POOL_GROUP, POOL_GROUP), POOL_GROUP ** -0.5),
        'pool_scale': 1.0 + nrm((L, D_POOL), 0.05),
        'w_branch_pool': nrm((L, D_POOL, D), D_POOL ** -0.5),
        'w_out': nrm((L, D, D), BETA * D ** -0.5),
        'ln1_w': 1.0 + nrm((L, D), 0.05),
        'ln1_b': nrm((L, D), 0.01),
        'ffn_up': nrm((L, D, 2 * D_FF), D ** -0.5),
        'conv_w': nrm((L, CONV_W, 2 * D_FF), CONV_W ** -0.5),
        'conv_b': nrm((L, 2 * D_FF), 0.01),
        'ffn_down': nrm((L, D_FF, D), BETA * D_FF ** -0.5),
        'ln2_w': 1.0 + nrm((L, D), 0.05),
        'ln2_b': nrm((L, D), 0.01),
    }


def reference(x_prompt, x_sample, state_shift, state_wkv, state_pool, state_conv, c_prompt, c_sample,
              ada_w, ada_b, w_in, mu_shift, decay_w0, decay_up, aaa_a0, aaa_up, gate_up, k_k, k_a, r_k,
              gn_w, gn_b, w_branch_rwkv, pool_w, pool_scale, w_branch_pool, w_out, ln1_w, ln1_b,
              ffn_up, conv_w, conv_b, ffn_down, ln2_w, ln2_b):
    p = {'ada_w': ada_w, 'ada_b': ada_b, 'w_in': w_in, 'mu_shift': mu_shift, 'decay_w0': decay_w0,
         'decay_up': decay_up, 'aaa_a0': aaa_a0, 'aaa_up': aaa_up, 'gate_up': gate_up, 'k_k': k_k,
         'k_a': k_a, 'r_k': r_k, 'gn_w': gn_w, 'gn_b': gn_b, 'w_branch_rwkv': w_branch_rwkv,
         'pool_w': pool_w, 'pool_scale': pool_scale, 'w_branch_pool': w_branch_pool, 'w_out': w_out,
         'ln1_w': ln1_w, 'ln1_b': ln1_b, 'ffn_up': ffn_up, 'conv_w': conv_w, 'conv_b': conv_b,
         'ffn_down': ffn_down, 'ln2_w': ln2_w, 'ln2_b': ln2_b}
    B = x_prompt.shape[0]
    f32 = jnp.float32
    z_shift = jnp.zeros((DEPTH, B, D_MODEL), f32)
    z_wkv = jnp.zeros((DEPTH, B, N_HEADS, HEAD_SIZE, HEAD_SIZE), f32)
    z_pool = jnp.zeros((DEPTH, B, POOL_BUF, D_POOL), f32)
    z_conv = jnp.zeros((DEPTH, B, CONV_W - 1, 2 * D_FF), f32)
    y_p, shift_p, wkv_p, pool_p, conv_p = run_trunk(x_prompt, c_prompt, 0, z_shift, z_wkv, z_pool, z_conv, p)
    y_s, shift_s, wkv_s, pool_s, conv_s = run_trunk(x_sample, c_sample, PAST_LEN, state_shift, state_wkv,
                                                    state_pool, state_conv, p)
    dt = x_prompt.dtype
    return (y_p.astype(dt), y_s.astype(dt), shift_p.astype(dt), wkv_p.astype(dt), pool_p.astype(dt),
            conv_p.astype(dt), shift_s.astype(dt), wkv_s.astype(dt), pool_s.astype(dt), conv_s.astype(dt))
```

```python
import functools
import math

import jax
import jax.numpy as jnp
from jax import lax
from jax.experimental import pallas as pl
from jax.experimental.pallas import tpu as pltpu

F32 = jnp.float32
BF16 = jnp.bfloat16

D_MODEL = 1024
HEAD_SIZE = 64
N_HEADS = D_MODEL // HEAD_SIZE
D_RWKV = N_HEADS * HEAD_SIZE
D_DECAY_LORA = 64
D_AAA_LORA = 64
D_GATE_LORA = 128
D_LORA = D_DECAY_LORA + D_AAA_LORA + D_GATE_LORA
POOL_WINDOWS = (2, 4, 8, 16)
POOL_GROUP = D_MODEL // 8
D_POOL = len(POOL_WINDOWS) * POOL_GROUP
POOL_BUF = max(POOL_WINDOWS) - 1
D_FF = (D_MODEL * 11) // 4
CONV_W = 3
D_SHIFT = 3 * D_RWKV + D_LORA
IN_COLS = D_SHIFT + D_POOL + 2 * D_MODEL
LN_EPS = 1e-5
GN_EPS = 64e-5
EXP_M05 = math.exp(-0.5)

VMEM_LIMIT_BYTES = 56 * 1024 * 1024
ROW_TILE = 256
CHUNK = 64
SAMPLE_TILE = 16
FF_CHUNK = 1408


def _ln(x):
    mu = jnp.mean(x, axis=-1, keepdims=True)
    xc = x - mu
    var = jnp.mean(xc * xc, axis=-1, keepdims=True)
    return xc * lax.rsqrt(var + LN_EPS)


def _dot(a, b):
    return jnp.dot(a.astype(BF16), b.astype(BF16), preferred_element_type=F32)


def _dot_nt(a, b):
    return lax.dot_general(a.astype(BF16), b.astype(BF16), (((1,), (1,)), ((), ())),
                           preferred_element_type=F32)


def _dot_tn(a, b):
    return lax.dot_general(a.astype(BF16), b.astype(BF16), (((0,), (0,)), ((), ())),
                           preferred_element_type=F32)


def _const_spec(shape):
    nd = len(shape)
    return pl.BlockSpec(shape, lambda *_: (0,) * nd, pipeline_mode=pl.Buffered(1))


def _params(sem):
    return pltpu.CompilerParams(dimension_semantics=sem, vmem_limit_bytes=VMEM_LIMIT_BYTES)


def _mod_kernel(c_ref, w_ref, b_ref, o_ref):
    o_ref[...] = _dot(c_ref[...], w_ref[...]) + b_ref[...]


def _modulation(c_all, ada_w, ada_b):
    depth, d, n = ada_w.shape
    rows = c_all.shape[0]
    tn = 1536
    return pl.pallas_call(
        _mod_kernel,
        grid=(depth, n // tn),
        in_specs=[pl.BlockSpec((rows, d), lambda l, j: (0, 0)),
                  pl.BlockSpec((None, d, tn), lambda l, j: (l, 0, j)),
                  pl.BlockSpec((None, 1, tn), lambda l, j: (l, 0, j))],
        out_specs=pl.BlockSpec((None, rows, tn), lambda l, j: (l, 0, j)),
        out_shape=jax.ShapeDtypeStruct((depth, rows, n), F32),
        compiler_params=_params(("arbitrary", "arbitrary")),
    )(c_all, ada_w, ada_b.reshape(depth, 1, n))


def _rwkv_prep(xs_r, xs_k, xs_v, lora, wl_ref, w0_ref, a0_ref, kk_ref, ka_ref, mhead_ref):
    lane = lax.broadcasted_iota(jnp.int32, lora.shape, 1)
    act = jnp.where(lane < D_DECAY_LORA, jnp.tanh(lora),
                    jnp.where(lane < D_DECAY_LORA + D_AAA_LORA, lora, jax.nn.sigmoid(lora)))
    z = _dot(act, wl_ref[...])
    dec = w0_ref[...] + z[:, 0:D_RWKV]
    lw = -jax.nn.sigmoid(dec) * EXP_M05
    a = jax.nn.sigmoid(a0_ref[...] + z[:, D_RWKV:2 * D_RWKV])
    g = z[:, 2 * D_RWKV:3 * D_RWKV]
    kk = xs_k * kk_ref[...]
    ss = _dot(kk * kk, mhead_ref[...])
    kap = kk / jnp.maximum(jnp.sqrt(ss), 1e-12)
    kmod = xs_k * (1.0 + (a - 1.0) * ka_ref[...])
    return xs_r, kmod, xs_v, kap, kap * a, lw, g


def _pool_tail(pooled, p_in, gate_pool, poolw_ref, pscale_ref, wbp_ref):
    mixed = []
    for gi in range(len(POOL_WINDOWS)):
        sl = slice(gi * POOL_GROUP, (gi + 1) * POOL_GROUP)
        diff = pooled[gi] - p_in[:, sl]
        mixed.append(_dot(diff, poolw_ref[gi]) * pscale_ref[:, sl])
    y_pool = _dot(jnp.concatenate(mixed, axis=-1), wbp_ref[...])
    return jax.nn.sigmoid(gate_pool) * y_pool


def _mix_in_prompt_kernel(x_ref, mod_ref, win_ref, mu_ref, wl_ref, w0_ref, a0_ref, kk_ref, ka_ref,
                          mhead_ref, poolw_ref, pscale_ref, wbp_ref,
                          r_ref, k_ref, v_ref, kap_ref, b_ref, lw_ref, g_ref, gpool_ref, sgr_ref,
                          shift_ref, pool_ref,
                          carry_ref, pext_ref, *, tm):
    j = pl.program_id(1)
    first = j == 0
    x = x_ref[...]
    h = _ln(x) * (1.0 + mod_ref[:, D_MODEL:2 * D_MODEL]) + mod_ref[:, 0:D_MODEL]
    shift_ref[...] = h[tm - 1:tm, :]
    hb = h.astype(BF16)
    row = lax.broadcasted_iota(jnp.int32, (tm, 1), 0)

    @pl.when(first)
    def _():
        carry_ref[...] = jnp.zeros(carry_ref.shape, F32)

    def shifted(c0, c1):
        cur = jnp.dot(hb, win_ref[:, c0:c1], preferred_element_type=F32)
        prev = jnp.where(row == 0, carry_ref[7:8, c0:c1], pltpu.roll(cur, 1, 0))
        carry_ref[7:8, c0:c1] = cur[tm - 1:tm, :]
        return cur + (prev - cur) * mu_ref[:, c0:c1]

    xs_r = shifted(0, D_RWKV)
    xs_k = shifted(D_RWKV, 2 * D_RWKV)
    xs_v = shifted(2 * D_RWKV, 3 * D_RWKV)
    lora = shifted(3 * D_RWKV, D_SHIFT)
    r, kmod, v, kap, b, lw, g = _rwkv_prep(xs_r, xs_k, xs_v, lora, wl_ref, w0_ref, a0_ref, kk_ref, ka_ref,
                                           mhead_ref)
    r_ref[...] = r
    k_ref[...] = kmod
    v_ref[...] = v
    kap_ref[...] = kap
    b_ref[...] = b
    lw_ref[...] = lw
    g_ref[...] = g.astype(BF16)

    p_in = jnp.dot(hb, win_ref[:, D_SHIFT:D_SHIFT + D_POOL], preferred_element_type=F32)

    @pl.when(first)
    def _():
        pext_ref[0:16, :] = jnp.zeros((16, D_POOL), F32)

    @pl.when(jnp.logical_not(first))
    def _():
        pext_ref[0:16, :] = pext_ref[tm:tm + 16, :]

    pext_ref[16:16 + tm, :] = p_in
    pos1 = (j * tm + row + 1).astype(F32)
    pooled = []
    for gi, win in enumerate(POOL_WINDOWS):
        sl = slice(gi * POOL_GROUP, (gi + 1) * POOL_GROUP)
        acc = p_in[:, sl]
        for d in range(1, win):
            acc = acc + pext_ref[16 - d:16 - d + tm, sl]
        pooled.append(acc / jnp.minimum(float(win), pos1))
    gates = jnp.dot(hb, win_ref[:, D_SHIFT + D_POOL:IN_COLS], preferred_element_type=F32)
    gpool = _pool_tail(pooled, p_in, gates[:, 0:D_MODEL], poolw_ref, pscale_ref, wbp_ref)
    gpool_ref[...] = gpool.astype(BF16)
    sgr_ref[...] = jax.nn.sigmoid(gates[:, D_MODEL:2 * D_MODEL]).astype(BF16)
    pool_ref[...] = pext_ref[tm:tm + 16, :]


def _mix_in_sample_kernel(x_ref, mod_ref, sprev_ref, pst_ref, win_ref, mu_ref, wl_ref, w0_ref, a0_ref, kk_ref,
                          ka_ref, mhead_ref, poolw_ref, pscale_ref, wbp_ref,
                          r_ref, k_ref, v_ref, kap_ref, b_ref, w_ref, g_ref, gpool_ref, sgr_ref,
                          shift_ref, pool_ref):
    x = x_ref[...]
    h = _ln(x) * (1.0 + mod_ref[:, D_MODEL:2 * D_MODEL]) + mod_ref[:, 0:D_MODEL]
    shift_ref[...] = h
    hb = h.astype(BF16)
    pb = sprev_ref[...].astype(BF16)

    def shifted(c0, c1):
        w = win_ref[:, c0:c1]
        cur = jnp.dot(hb, w, preferred_element_type=F32)
        prev = jnp.dot(pb, w, preferred_element_type=F32)
        return cur + (prev - cur) * mu_ref[:, c0:c1]

    xs_r = shifted(0, D_RWKV)
    xs_k = shifted(D_RWKV, 2 * D_RWKV)
    xs_v = shifted(2 * D_RWKV, 3 * D_RWKV)
    lora = shifted(3 * D_RWKV, D_SHIFT)
    r, kmod, v, kap, b, lw, g = _rwkv_prep(xs_r, xs_k, xs_v, lora, wl_ref, w0_ref, a0_ref, kk_ref, ka_ref,
                                           mhead_ref)
    r_ref[...] = r
    k_ref[...] = kmod
    v_ref[...] = v
    kap_ref[...] = kap
    b_ref[...] = b
    w_ref[...] = jnp.exp(lw)
    g_ref[...] = g

    p_in = jnp.dot(hb, win_ref[:, D_SHIFT:D_SHIFT + D_POOL], preferred_element_type=F32)
    pooled = []
    for gi, win in enumerate(POOL_WINDOWS):
        sl = slice(gi * POOL_GROUP, (gi + 1) * POOL_GROUP)
        acc = p_in[:, sl]
        for d in range(1, win):
            acc = acc + pst_ref[POOL_BUF - d, :, sl]
        pooled.append(acc / float(win))
    gates = jnp.dot(hb, win_ref[:, D_SHIFT + D_POOL:IN_COLS], preferred_element_type=F32)
    gpool = _pool_tail(pooled, p_in, gates[:, 0:D_MODEL], poolw_ref, pscale_ref, wbp_ref)
    gpool_ref[...] = gpool.astype(BF16)
    sgr_ref[...] = jax.nn.sigmoid(gates[:, D_MODEL:2 * D_MODEL]).astype(BF16)
    for i in range(POOL_BUF - 1):
        pool_ref[i] = pst_ref[i + 1]
    pool_ref[POOL_BUF - 1] = p_in


def _mix_in_weight_specs():
    return [_const_spec((D_MODEL, IN_COLS)),
            _const_spec((1, D_SHIFT)),
            _const_spec((D_LORA, 3 * D_RWKV)),
            _const_spec((1, D_RWKV)),
            _const_spec((1, D_RWKV)),
            _const_spec((1, D_RWKV)),
            _const_spec((1, D_RWKV)),
            _const_spec((D_RWKV, D_RWKV)),
            _const_spec((len(POOL_WINDOWS), POOL_GROUP, POOL_GROUP)),
            _const_spec((1, D_POOL)),
            _const_spec((D_POOL, D_MODEL))]


def _mix_in_prompt(x2d, mod3, wts, batch, seq):
    tm = min(ROW_TILE, seq)
    nj = seq // tm
    rows = batch * seq
    row_spec = pl.BlockSpec((tm, D_MODEL), lambda b, j: (b * nj + j, 0))
    f32_rows = jax.ShapeDtypeStruct((rows, D_MODEL), F32)
    bf_rows = jax.ShapeDtypeStruct((rows, D_MODEL), BF16)
    return pl.pallas_call(
        functools.partial(_mix_in_prompt_kernel, tm=tm),
        grid=(batch, nj),
        in_specs=[row_spec, pl.BlockSpec((None, 1, 6 * D_MODEL), lambda b, j: (b, 0, 0))] + _mix_in_weight_specs(),
        out_specs=[row_spec] * 9 + [pl.BlockSpec((None, 1, D_MODEL), lambda b, j: (b, 0, 0)),
                                    pl.BlockSpec((None, 16, D_POOL), lambda b, j: (b, 0, 0))],
        out_shape=[f32_rows] * 6 + [bf_rows] * 3 + [jax.ShapeDtypeStruct((batch, 1, D_MODEL), F32),
                                                    jax.ShapeDtypeStruct((batch, 16, D_POOL), F32)],
        scratch_shapes=[pltpu.VMEM((8, D_SHIFT), F32), pltpu.VMEM((tm + 16, D_POOL), F32)],
        compiler_params=_params(("arbitrary", "arbitrary")),
    )(x2d, mod3, *wts)


def _mix_in_sample(x2d, mod2, sprev, pool_t, wts):
    n = x2d.shape[0]
    full = lambda shape: pl.BlockSpec(shape, lambda i: (0,) * len(shape), pipeline_mode=pl.Buffered(1))
    out_full = lambda shape: pl.BlockSpec(shape, lambda i: (0,) * len(shape))
    f32_rows = jax.ShapeDtypeStruct((n, D_MODEL), F32)
    bf_rows = jax.ShapeDtypeStruct((n, D_MODEL), BF16)
    return pl.pallas_call(
        _mix_in_sample_kernel,
        grid=(1,),
        in_specs=[full((n, D_MODEL)), full((n, 6 * D_MODEL)), full((n, D_MODEL)), full((POOL_BUF, n, D_POOL))]
        + _mix_in_weight_specs(),
        out_specs=[out_full((n, D_MODEL))] * 10 + [out_full((POOL_BUF, n, D_POOL))],
        out_shape=[f32_rows] * 7 + [bf_rows] * 2 + [f32_rows, jax.ShapeDtypeStruct((POOL_BUF, n, D_POOL), F32)],
        compiler_params=_params(("arbitrary",)),
    )(x2d, mod2, sprev, pool_t, *wts)


def _head_epilogue(y, r, kmod, v, g, rk, gnw, gnb):
    mu = jnp.mean(y, axis=-1, keepdims=True)
    yc = y - mu
    var = jnp.mean(yc * yc, axis=-1, keepdims=True)
    yn = yc * lax.rsqrt(var + GN_EPS) * gnw + gnb
    bonus = jnp.sum(r * kmod * rk, axis=-1, keepdims=True) * v
    return (yn + bonus) * g


def _wkv_prompt_kernel(r_ref, k_ref, v_ref, kap_ref, b_ref, lw_ref, g_ref, rk_ref, gnw_ref, gnb_ref, tri_ref,
                       yg_ref, snew_ref, s_ref):
    c = pl.program_id(1)

    @pl.when(c == 0)
    def _():
        s_ref[...] = jnp.zeros(s_ref.shape, F32)

    L = CHUNK
    lw = lw_ref[...]
    p_inc = jnp.dot(tri_ref[...], lw, precision=lax.Precision.HIGHEST, preferred_element_type=F32)
    p_last = p_inc[L - 1:L, :]
    e_p = jnp.exp(p_inc)
    e_np = jnp.exp(-p_inc)
    e_e = jnp.exp(p_inc - lw)
    e_l = jnp.exp(p_last - p_inc)
    dec_last = jnp.exp(p_last)
    r = r_ref[...]
    kmod = k_ref[...]
    v = v_ref[...]
    b = b_ref[...]
    g = g_ref[...].astype(F32)
    r_hat = r * e_p
    a_hat = -kap_ref[...] * e_e
    b_hat = b * e_np
    k_hat = kmod * e_np
    b_til = b * e_l
    k_til = kmod * e_l

    row = lax.broadcasted_iota(jnp.int32, (L, 2 * L), 0)
    col = lax.broadcasted_iota(jnp.int32, (L, 2 * L), 1)
    src = jnp.where(col < L, col, col - L)
    strict = src < row
    incl = src <= row
    right = col >= L
    eye = (lax.broadcasted_iota(jnp.int32, (L, L), 0) == lax.broadcasted_iota(jnp.int32, (L, L), 1)).astype(F32)

    for hd in range(N_HEADS):
        sl = slice(hd * HEAD_SIZE, (hd + 1) * HEAD_SIZE)
        ar = jnp.concatenate([a_hat[:, sl], r_hat[:, sl]], axis=0)
        bk = jnp.concatenate([b_hat[:, sl], k_hat[:, sl]], axis=0)
        gram = _dot_nt(ar, bk)
        g_top = jnp.where(strict, gram[0:L], 0.0)
        g_bot = jnp.where(incl, gram[L:2 * L], 0.0)
        nil = g_top[:, 0:L]
        inv = eye + nil
        pw = nil
        for _ in range(5):
            pw = _dot(pw, pw)
            inv = inv + _dot(inv, pw)
        vh = v[:, sl]
        vv = jnp.concatenate([vh, vh], axis=0)
        akv = _dot(jnp.where(right, g_top, 0.0), vv)
        s_in = s_ref[hd]
        ars = _dot_nt(ar, s_in)
        u = _dot(inv, ars[0:L] + akv)
        uv = jnp.concatenate([u, vh], axis=0)
        y = ars[L:2 * L] + _dot(g_bot, uv)
        s_new = s_in * dec_last[:, sl] + _dot_tn(uv, jnp.concatenate([b_til[:, sl], k_til[:, sl]], axis=0))
        s_ref[hd] = s_new
        snew_ref[hd] = s_new
        out = _head_epilogue(y, r[:, sl], kmod[:, sl], vh, g[:, sl], rk_ref[:, sl], gnw_ref[:, sl], gnb_ref[:, sl])
        yg_ref[:, sl] = out.astype(BF16)


def _wkv_prompt(ops, rk, gnw, gnb, batch, seq):
    nc = seq // CHUNK
    rows = batch * seq
    row_spec = pl.BlockSpec((CHUNK, D_RWKV), lambda b, c: (b * nc + c, 0))
    tri = jnp.tril(jnp.ones((CHUNK, CHUNK), F32))
    return pl.pallas_call(
        _wkv_prompt_kernel,
        grid=(batch, nc),
        in_specs=[row_spec] * 7 + [_const_spec((1, D_RWKV))] * 3 + [_const_spec((CHUNK, CHUNK))],
        out_specs=[row_spec, pl.BlockSpec((None, N_HEADS, HEAD_SIZE, HEAD_SIZE), lambda b, c: (b, 0, 0, 0))],
        out_shape=[jax.ShapeDtypeStruct((rows, D_RWKV), BF16),
                   jax.ShapeDtypeStruct((batch, N_HEADS, HEAD_SIZE, HEAD_SIZE), F32)],
        scratch_shapes=[pltpu.VMEM((N_HEADS, HEAD_SIZE, HEAD_SIZE), F32)],
        compiler_params=_params(("arbitrary", "arbitrary")),
    )(*ops, rk, gnw, gnb, tri)


def _wkv_sample_kernel(r_ref, k_ref, v_ref, kap_ref, b_ref, w_ref, g_ref, rk_ref, gnw_ref, gnb_ref, s_ref,
                       yg_ref, snew_ref):
    n = HEAD_SIZE
    eye = lax.broadcasted_iota(jnp.int32, (n, n), 0) == lax.broadcasted_iota(jnp.int32, (n, n), 1)

    def body(i, carry):
        rows = pl.ds(i, 1)
        r_row, k_row, v_row, kap_row = r_ref[rows, :], k_ref[rows, :], v_ref[rows, :], kap_ref[rows, :]
        b_row, w_row, g_row = b_ref[rows, :], w_ref[rows, :], g_ref[rows, :]
        outs = []
        for hd in range(N_HEADS):
            sl = slice(hd * n, (hd + 1) * n)
            s_in = s_ref[i, hd]
            r, kmod, v = r_row[:, sl], k_row[:, sl], v_row[:, sl]
            sa = jnp.sum(s_in * kap_row[:, sl], axis=-1, keepdims=True)
            vcol = jnp.sum(jnp.where(eye, v, 0.0), axis=-1, keepdims=True)
            s_new = s_in * w_row[:, sl] - sa * b_row[:, sl] + vcol * kmod
            snew_ref[i, hd] = s_new
            ycol = jnp.sum(s_new * r, axis=-1, keepdims=True)
            y = jnp.sum(jnp.where(eye, ycol, 0.0), axis=0, keepdims=True)
            outs.append(_head_epilogue(y, r, kmod, v, g_row[:, sl], rk_ref[:, sl], gnw_ref[:, sl], gnb_ref[:, sl]))
        yg_ref[rows, :] = jnp.concatenate(outs, axis=-1)
        return carry

    lax.fori_loop(0, SAMPLE_TILE, body, 0)


def _wkv_sample(ops, rk, gnw, gnb, state):
    n = state.shape[0]
    bt = min(SAMPLE_TILE, n)
    row_spec = pl.BlockSpec((bt, D_RWKV), lambda i: (i, 0))
    st_spec = pl.BlockSpec((bt, N_HEADS, HEAD_SIZE, HEAD_SIZE), lambda i: (i, 0, 0, 0))
    return pl.pallas_call(
        _wkv_sample_kernel,
        grid=(n // bt,),
        in_specs=[row_spec] * 7 + [_const_spec((1, D_RWKV))] * 3 + [st_spec],
        out_specs=[row_spec, st_spec],
        out_shape=[jax.ShapeDtypeStruct((n, D_RWKV), F32), jax.ShapeDtypeStruct(state.shape, F32)],
        compiler_params=_params(("arbitrary",)),
    )(*ops, rk, gnw, gnb, state)


def _mix_out_kernel(yg_ref, gpool_ref, sgr_ref, x_ref, mod_ref, wbr_ref, wout_ref, lnw_ref, lnb_ref, o_ref, *, alpha):
    y_rwkv = _dot(yg_ref[...], wbr_ref[...])
    merged = gpool_ref[...].astype(F32) + sgr_ref[...].astype(F32) * y_rwkv
    mix = _dot(merged, wout_ref[...])
    g1 = mod_ref[:, 2 * D_MODEL:3 * D_MODEL]
    o_ref[...] = _ln(alpha * x_ref[...] + (1.0 + g1) * mix) * lnw_ref[...] + lnb_ref[...]


def _mix_out(yg, gpool, sgr, x2d, mod, mod_spec, grid, row_map, tm, wts, alpha):
    row_spec = pl.BlockSpec((tm, D_MODEL), row_map)
    return pl.pallas_call(
        functools.partial(_mix_out_kernel, alpha=alpha),
        grid=grid,
        in_specs=[row_spec] * 4 + [mod_spec, _const_spec((D_RWKV, D_MODEL)), _const_spec((D_MODEL, D_MODEL)),
                                  _const_spec((1, D_MODEL)), _const_spec((1, D_MODEL))],
        out_specs=row_spec,
        out_shape=jax.ShapeDtypeStruct(x2d.shape, F32),
        compiler_params=_params(("arbitrary",) * len(grid)),
    )(yg, gpool, sgr, x2d, mod, *wts)


def _ffn_prompt_kernel(x_ref, mod_ref, up_ref, cw_ref, cb_ref, down_ref, lnw_ref, lnb_ref,
                       o_ref, conv_ref, carry_ref, *, tm, alpha):
    j = pl.program_id(1)
    first = j == 0
    x = x_ref[...]
    h2 = _ln(x) * (1.0 + mod_ref[:, 4 * D_MODEL:5 * D_MODEL]) + mod_ref[:, 3 * D_MODEL:4 * D_MODEL]
    hb = h2.astype(BF16)
    row = lax.broadcasted_iota(jnp.int32, (tm, 1), 0)

    @pl.when(first)
    def _():
        carry_ref[...] = jnp.zeros(carry_ref.shape, F32)

    def conv(c0, c1):
        u = jnp.dot(hb, up_ref[:, c0:c1], preferred_element_type=F32)
        c6 = carry_ref[6:7, c0:c1]
        c7 = carry_ref[7:8, c0:c1]
        p1 = jnp.where(row == 0, c7, pltpu.roll(u, 1, 0))
        p2 = jnp.where(row == 0, c6, jnp.where(row == 1, c7, pltpu.roll(u, 2, 0)))
        carry_ref[6:8, c0:c1] = u[tm - 2:tm, :]
        conv_ref[:, c0:c1] = u[tm - 2:tm, :]
        return cb_ref[:, c0:c1] + p2 * cw_ref[0:1, c0:c1] + p1 * cw_ref[1:2, c0:c1] + u * cw_ref[2:3, c0:c1]

    ff = jnp.zeros((tm, D_MODEL), F32)
    for c in range(D_FF // FF_CHUNK):
        val = conv(c * FF_CHUNK, (c + 1) * FF_CHUNK)
        gate = conv(D_FF + c * FF_CHUNK, D_FF + (c + 1) * FF_CHUNK)
        ff = ff + _dot(val * jax.nn.silu(gate), down_ref[c * FF_CHUNK:(c + 1) * FF_CHUNK, :])
    g2 = mod_ref[:, 5 * D_MODEL:6 * D_MODEL]
    o_ref[...] = _ln(alpha * x + (1.0 + g2) * ff) * lnw_ref[...] + lnb_ref[...]


def _ffn_weight_specs():
    return [_const_spec((D_MODEL, 2 * D_FF)), _const_spec((CONV_W, 2 * D_FF)), _const_spec((1, 2 * D_FF)),
            _const_spec((D_FF, D_MODEL)), _const_spec((1, D_MODEL)), _const_spec((1, D_MODEL))]


def _ffn_prompt(x2d, mod3, wts, batch, seq, alpha):
    tm = min(ROW_TILE, seq)
    nj = seq // tm
    row_spec = pl.BlockSpec((tm, D_MODEL), lambda b, j: (b * nj + j, 0))
    return pl.pallas_call(
        functools.partial(_ffn_prompt_kernel, tm=tm, alpha=alpha),
        grid=(batch, nj),
        in_specs=[row_spec, pl.BlockSpec((None, 1, 6 * D_MODEL), lambda b, j: (b, 0, 0))] + _ffn_weight_specs(),
        out_specs=[row_spec, pl.BlockSpec((None, CONV_W - 1, 2 * D_FF), lambda b, j: (b, 0, 0))],
        out_shape=[jax.ShapeDtypeStruct(x2d.shape, F32),
                   jax.ShapeDtypeStruct((batch, CONV_W - 1, 2 * D_FF), F32)],
        scratch_shapes=[pltpu.VMEM((8, 2 * D_FF), F32)],
        compiler_params=_params(("arbitrary", "arbitrary")),
    )(x2d, mod3, *wts)


def _ffn_sample_kernel(x_ref, mod_ref, cst_ref, up_ref, cw_ref, cb_ref, down_ref, lnw_ref, lnb_ref,
                       o_ref, conv_ref, *, alpha):
    x = x_ref[...]
    h2 = _ln(x) * (1.0 + mod_ref[:, 4 * D_MODEL:5 * D_MODEL]) + mod_ref[:, 3 * D_MODEL:4 * D_MODEL]
    hb = h2.astype(BF16)

    def conv(c0, c1):
        u = jnp.dot(hb, up_ref[:, c0:c1], preferred_element_type=F32)
        p2 = cst_ref[0, :, c0:c1]
        p1 = cst_ref[1, :, c0:c1]
        conv_ref[0, :, c0:c1] = p1
        conv_ref[1, :, c0:c1] = u
        return cb_ref[:, c0:c1] + p2 * cw_ref[0:1, c0:c1] + p1 * cw_ref[1:2, c0:c1] + u * cw_ref[2:3, c0:c1]

    ff = jnp.zeros(x.shape, F32)
    for c in range(D_FF // FF_CHUNK):
        val = conv(c * FF_CHUNK, (c + 1) * FF_CHUNK)
        gate = conv(D_FF + c * FF_CHUNK, D_FF + (c + 1) * FF_CHUNK)
        ff = ff + _dot(val * jax.nn.silu(gate), down_ref[c * FF_CHUNK:(c + 1) * FF_CHUNK, :])
    g2 = mod_ref[:, 5 * D_MODEL:6 * D_MODEL]
    o_ref[...] = _ln(alpha * x + (1.0 + g2) * ff) * lnw_ref[...] + lnb_ref[...]


def _ffn_sample(x2d, mod2, conv_t, wts, alpha):
    n = x2d.shape[0]
    full = lambda shape: pl.BlockSpec(shape, lambda i: (0,) * len(shape), pipeline_mode=pl.Buffered(1))
    out_full = lambda shape: pl.BlockSpec(shape, lambda i: (0,) * len(shape))
    return pl.pallas_call(
        functools.partial(_ffn_sample_kernel, alpha=alpha),
        grid=(1,),
        in_specs=[full((n, D_MODEL)), full((n, 6 * D_MODEL)), full((CONV_W - 1, n, 2 * D_FF))] + _ffn_weight_specs(),
        out_specs=[out_full((n, D_MODEL)), out_full((CONV_W - 1, n, 2 * D_FF))],
        out_shape=[jax.ShapeDtypeStruct(x2d.shape, F32), jax.ShapeDtypeStruct((CONV_W - 1, n, 2 * D_FF), F32)],
        compiler_params=_params(("arbitrary",)),
    )(x2d, mod2, conv_t, *wts)


def _layer_weights(l, p):
    row = lambda a: a[l].reshape(1, -1)
    bf = lambda a: a[l].astype(BF16)
    lora_up = jnp.zeros((D_LORA, 3 * D_RWKV), F32)
    lora_up = lora_up.at[0:D_DECAY_LORA, 0:D_RWKV].set(p['decay_up'][l])
    lora_up = lora_up.at[D_DECAY_LORA:D_DECAY_LORA + D_AAA_LORA, D_RWKV:2 * D_RWKV].set(p['aaa_up'][l])
    lora_up = lora_up.at[D_DECAY_LORA + D_AAA_LORA:, 2 * D_RWKV:].set(p['gate_up'][l])
    head = jnp.arange(D_RWKV) // HEAD_SIZE
    mhead = (head[:, None] == head[None, :]).astype(BF16)
    mix_in = (bf(p['w_in']), row(p['mu_shift']), lora_up.astype(BF16), row(p['decay_w0']), row(p['aaa_a0']),
              row(p['k_k']), row(p['k_a']), mhead, bf(p['pool_w']), row(p['pool_scale']), bf(p['w_branch_pool']))
    wkv = (row(p['r_k']), row(p['gn_w']), row(p['gn_b']))
    mix_out = (bf(p['w_branch_rwkv']), bf(p['w_out']), row(p['ln1_w']), row(p['ln1_b']))
    ffn = (bf(p['ffn_up']), p['conv_w'][l], row(p['conv_b']), bf(p['ffn_down']), row(p['ln2_w']), row(p['ln2_b']))
    return mix_in, wkv, mix_out, ffn


def _trunk(x_prompt, x_sample, state_shift, state_wkv, state_pool, state_conv, mod, p):
    depth = p['w_in'].shape[0]
    alpha = float((2 * depth) ** 0.25)
    batch, seq, _ = x_prompt.shape
    nsamp = x_sample.shape[0]
    xp = x_prompt.reshape(batch * seq, D_MODEL).astype(F32)
    xs = x_sample.reshape(nsamp, D_MODEL).astype(F32)
    tm = min(ROW_TILE, seq)
    nj = seq // tm
    prompt_mod_spec = pl.BlockSpec((None, 1, 6 * D_MODEL), lambda i: (i // nj, 0, 0))
    sample_mod_spec = pl.BlockSpec((nsamp, 6 * D_MODEL), lambda i: (0, 0))
    outs = {k: [] for k in ('shift_p', 'wkv_p', 'pool_p', 'conv_p', 'shift_s', 'wkv_s', 'pool_s', 'conv_s')}
    for l in range(depth):
        w_mix_in, w_wkv, w_mix_out, w_ffn = _layer_weights(l, p)
        mod_p = mod[l, 0:batch].reshape(batch, 1, 6 * D_MODEL)
        mod_s = mod[l, batch:batch + nsamp]

        *ops, gpool, sgr, shift_p, pool_p = _mix_in_prompt(xp, mod_p, w_mix_in, batch, seq)
        yg, wkv_p = _wkv_prompt(ops, *w_wkv, batch, seq)
        xp = _mix_out(yg, gpool, sgr, xp, mod_p, prompt_mod_spec, (batch * nj,), lambda i: (i, 0), tm,
                      w_mix_out, alpha)
        xp, conv_p = _ffn_prompt(xp, mod_p, w_ffn, batch, seq, alpha)
        outs['shift_p'].append(shift_p.reshape(batch, D_MODEL))
        outs['wkv_p'].append(wkv_p)
        outs['pool_p'].append(pool_p[:, 1:, :])
        outs['conv_p'].append(conv_p)

        pool_t = jnp.swapaxes(state_pool[l].astype(F32), 0, 1)
        conv_t = jnp.swapaxes(state_conv[l].astype(F32), 0, 1)
        *ops, gpool, sgr, shift_s, pool_s = _mix_in_sample(xs, mod_s, state_shift[l].astype(F32), pool_t, w_mix_in)
        yg, wkv_s = _wkv_sample(ops, *w_wkv, state_wkv[l].astype(F32))
        xs = _mix_out(yg, gpool, sgr, xs, mod_s, sample_mod_spec, (1,), lambda i: (0, 0), nsamp, w_mix_out, alpha)
        xs, conv_s = _ffn_sample(xs, mod_s, conv_t, w_ffn, alpha)
        outs['shift_s'].append(shift_s)
        outs['wkv_s'].append(wkv_s)
        outs['pool_s'].append(jnp.swapaxes(pool_s, 0, 1))
        outs['conv_s'].append(jnp.swapaxes(conv_s, 0, 1))
    st = {k: jnp.stack(v) for k, v in outs.items()}
    return (xp.reshape(batch, seq, D_MODEL), xs.reshape(nsamp, 1, D_MODEL), st['shift_p'], st['wkv_p'], st['pool_p'],
            st['conv_p'], st['shift_s'], st['wkv_s'], st['pool_s'], st['conv_s'])


def kernel(x_prompt, x_sample, state_shift, state_wkv, state_pool, state_conv, c_prompt, c_sample, ada_w, ada_b, w_in, mu_shift, decay_w0, decay_up, aaa_a0, aaa_up, gate_up, k_k, k_a, r_k, gn_w, gn_b, w_branch_rwkv, pool_w, pool_scale, w_branch_pool, w_out, ln1_w, ln1_b, ffn_up, conv_w, conv_b, ffn_down, ln2_w, ln2_b):
    p = {'w_in': w_in, 'mu_shift': mu_shift, 'decay_w0': decay_w0, 'decay_up': decay_up, 'aaa_a0': aaa_a0,
         'aaa_up': aaa_up, 'gate_up': gate_up, 'k_k': k_k, 'k_a': k_a, 'r_k': r_k, 'gn_w': gn_w, 'gn_b': gn_b,
         'w_branch_rwkv': w_branch_rwkv, 'pool_w': pool_w, 'pool_scale': pool_scale,
         'w_branch_pool': w_branch_pool, 'w_out': w_out, 'ln1_w': ln1_w, 'ln1_b': ln1_b, 'ffn_up': ffn_up,
         'conv_w': conv_w, 'conv_b': conv_b, 'ffn_down': ffn_down, 'ln2_w': ln2_w, 'ln2_b': ln2_b}
    c_all = jnp.concatenate([c_prompt, c_sample], axis=0).astype(F32)
    mod = _modulation(c_all, ada_w, ada_b)
    dt = x_prompt.dtype
    out = _trunk(x_prompt, x_sample, state_shift, state_wkv, state_pool, state_conv, mod, p)
    return tuple(o.astype(dt) for o in out)
```

```python
import functools
import math

import jax
import jax.numpy as jnp
from jax import lax
from jax.experimental import pallas as pl
from jax.experimental.pallas import tpu as pltpu

F32 = jnp.float32
BF16 = jnp.bfloat16

D_MODEL = 1024
HEAD_SIZE = 64
N_HEADS = D_MODEL // HEAD_SIZE
D_RWKV = N_HEADS * HEAD_SIZE
D_DECAY_LORA = 64
D_AAA_LORA = 64
D_GATE_LORA = 128
D_LORA = D_DECAY_LORA + D_AAA_LORA + D_GATE_LORA
POOL_WINDOWS = (2, 4, 8, 16)
POOL_GROUP = D_MODEL // 8
D_POOL = len(POOL_WINDOWS) * POOL_GROUP
POOL_BUF = max(POOL_WINDOWS) - 1
D_FF = (D_MODEL * 11) // 4
CONV_W = 3
D_SHIFT = 3 * D_RWKV + D_LORA
IN_COLS = D_SHIFT + D_POOL + 2 * D_MODEL
LN_EPS = 1e-5
GN_EPS = 64e-5
EXP_M05 = math.exp(-0.5)

VMEM_LIMIT_BYTES = 56 * 1024 * 1024
ROW_TILE = 256
CHUNK = 64
SAMPLE_TILE = 16
FF_CHUNK = 1408


def _ln(x):
    mu = jnp.mean(x, axis=-1, keepdims=True)
    xc = x - mu
    var = jnp.mean(xc * xc, axis=-1, keepdims=True)
    return xc * lax.rsqrt(var + LN_EPS)


def _dot(a, b):
    return jnp.dot(a.astype(BF16), b.astype(BF16), preferred_element_type=F32)


def _dot_nt(a, b):
    return lax.dot_general(a.astype(BF16), b.astype(BF16), (((1,), (1,)), ((), ())),
                           preferred_element_type=F32)


def _dot_tn(a, b):
    return lax.dot_general(a.astype(BF16), b.astype(BF16), (((0,), (0,)), ((), ())),
                           preferred_element_type=F32)


def _const_spec(shape):
    nd = len(shape)
    return pl.BlockSpec(shape, lambda *_: (0,) * nd, pipeline_mode=pl.Buffered(1))


def _params(sem):
    return pltpu.CompilerParams(dimension_semantics=sem, vmem_limit_bytes=VMEM_LIMIT_BYTES)


def _mod_kernel(c_ref, w_ref, b_ref, o_ref):
    o_ref[...] = _dot(c_ref[...], w_ref[...]) + b_ref[...]


def _modulation(c_all, ada_w, ada_b):
    depth, d, n = ada_w.shape
    rows = c_all.shape[0]
    tn = 1536
    return pl.pallas_call(
        _mod_kernel,
        grid=(depth, n // tn),
        in_specs=[pl.BlockSpec((rows, d), lambda l, j: (0, 0)),
                  pl.BlockSpec((None, d, tn), lambda l, j: (l, 0, j)),
                  pl.BlockSpec((None, 1, tn), lambda l, j: (l, 0, j))],
        out_specs=pl.BlockSpec((None, rows, tn), lambda l, j: (l, 0, j)),
        out_shape=jax.ShapeDtypeStruct((depth, rows, n), F32),
        compiler_params=_params(("arbitrary", "arbitrary")),
    )(c_all, ada_w, ada_b.reshape(depth, 1, n))


def _rwkv_prep(xs_r, xs_k, xs_v, lora, wl_ref, w0_ref, a0_ref, kk_ref, ka_ref, mhead_ref):
    lane = lax.broadcasted_iota(jnp.int32, lora.shape, 1)
    act = jnp.where(lane < D_DECAY_LORA, jnp.tanh(lora),
                    jnp.where(lane < D_DECAY_LORA + D_AAA_LORA, lora, jax.nn.sigmoid(lora)))
    z = _dot(act, wl_ref[...])
    dec = w0_ref[...] + z[:, 0:D_RWKV]
    lw = -jax.nn.sigmoid(dec) * EXP_M05
    a = jax.nn.sigmoid(a0_ref[...] + z[:, D_RWKV:2 * D_RWKV])
    g = z[:, 2 * D_RWKV:3 * D_RWKV]
    kk = xs_k * kk_ref[...]
    ss = _dot(kk * kk, mhead_ref[...])
    kap = kk / jnp.maximum(jnp.sqrt(ss), 1e-12)
    kmod = xs_k * (1.0 + (a - 1.0) * ka_ref[...])
    return xs_r, kmod, xs_v, kap, kap * a, lw, g


def _pool_tail(pooled, p_in, gate_pool, poolw_ref, pscale_ref, wbp_ref):
    mixed = []
    for gi in range(len(POOL_WINDOWS)):
        sl = slice(gi * POOL_GROUP, (gi + 1) * POOL_GROUP)
        diff = pooled[gi] - p_in[:, sl]
        mixed.append(_dot(diff, poolw_ref[gi]) * pscale_ref[:, sl])
    y_pool = _dot(jnp.concatenate(mixed, axis=-1), wbp_ref[...])
    return jax.nn.sigmoid(gate_pool) * y_pool


def _mix_in_prompt_kernel(x_ref, mod_ref, win_ref, mu_ref, wl_ref, w0_ref, a0_ref, kk_ref, ka_ref,
                          mhead_ref, poolw_ref, pscale_ref, wbp_ref,
                          r_ref, k_ref, v_ref, kap_ref, b_ref, lw_ref, g_ref, gpool_ref, sgr_ref,
                          shift_ref, pool_ref,
                          carry_ref, pext_ref, *, tm):
    j = pl.program_id(1)
    first = j == 0
    x = x_ref[...]
    h = _ln(x) * (1.0 + mod_ref[:, D_MODEL:2 * D_MODEL]) + mod_ref[:, 0:D_MODEL]
    shift_ref[...] = h[tm - 1:tm, :]
    hb = h.astype(BF16)
    row = lax.broadcasted_iota(jnp.int32, (tm, 1), 0)

    @pl.when(first)
    def _():
        carry_ref[...] = jnp.zeros(carry_ref.shape, F32)

    def shifted(c0, c1):
        cur = jnp.dot(hb, win_ref[:, c0:c1], preferred_element_type=F32)
        prev = jnp.where(row == 0, carry_ref[7:8, c0:c1], pltpu.roll(cur, 1, 0))
        carry_ref[7:8, c0:c1] = cur[tm - 1:tm, :]
        return cur + (prev - cur) * mu_ref[:, c0:c1]

    xs_r = shifted(0, D_RWKV)
    xs_k = shifted(D_RWKV, 2 * D_RWKV)
    xs_v = shifted(2 * D_RWKV, 3 * D_RWKV)
    lora = shifted(3 * D_RWKV, D_SHIFT)
    r, kmod, v, kap, b, lw, g = _rwkv_prep(xs_r, xs_k, xs_v, lora, wl_ref, w0_ref, a0_ref, kk_ref, ka_ref,
                                           mhead_ref)
    r_ref[...] = r
    k_ref[...] = kmod
    v_ref[...] = v
    kap_ref[...] = kap
    b_ref[...] = b
    lw_ref[...] = lw
    g_ref[...] = g.astype(BF16)

    p_in = jnp.dot(hb, win_ref[:, D_SHIFT:D_SHIFT + D_POOL], preferred_element_type=F32)

    @pl.when(first)
    def _():
        pext_ref[0:16, :] = jnp.zeros((16, D_POOL), F32)

    @pl.when(jnp.logical_not(first))
    def _():
        pext_ref[0:16, :] = pext_ref[tm:tm + 16, :]

    pext_ref[16:16 + tm, :] = p_in
    pos1 = (j * tm + row + 1).astype(F32)
    pooled = []
    for gi, win in enumerate(POOL_WINDOWS):
        sl = slice(gi * POOL_GROUP, (gi + 1) * POOL_GROUP)
        acc = p_in[:, sl]
        for d in range(1, win):
            acc = acc + pext_ref[16 - d:16 - d + tm, sl]
        pooled.append(acc / jnp.minimum(float(win), pos1))
    gates = jnp.dot(hb, win_ref[:, D_SHIFT + D_POOL:IN_COLS], preferred_element_type=F32)
    gpool = _pool_tail(pooled, p_in, gates[:, 0:D_MODEL], poolw_ref, pscale_ref, wbp_ref)
    gpool_ref[...] = gpool.astype(BF16)
    sgr_ref[...] = jax.nn.sigmoid(gates[:, D_MODEL:2 * D_MODEL]).astype(BF16)
    pool_ref[...] = pext_ref[tm:tm + 16, :]


def _mix_in_sample_kernel(x_ref, mod_ref, sprev_ref, pst_ref, win_ref, mu_ref, wl_ref, w0_ref, a0_ref, kk_ref,
                          ka_ref, mhead_ref, poolw_ref, pscale_ref, wbp_ref,
                          r_ref, k_ref, v_ref, kap_ref, b_ref, w_ref, g_ref, gpool_ref, sgr_ref,
                          shift_ref, pool_ref):
    x = x_ref[...]
    h = _ln(x) * (1.0 + mod_ref[:, D_MODEL:2 * D_MODEL]) + mod_ref[:, 0:D_MODEL]
    shift_ref[...] = h
    hb = h.astype(BF16)
    pb = sprev_ref[...].astype(BF16)

    def shifted(c0, c1):
        w = win_ref[:, c0:c1]
        cur = jnp.dot(hb, w, preferred_element_type=F32)
        prev = jnp.dot(pb, w, preferred_element_type=F32)
        return cur + (prev - cur) * mu_ref[:, c0:c1]

    xs_r = shifted(0, D_RWKV)
    xs_k = shifted(D_RWKV, 2 * D_RWKV)
    xs_v = shifted(2 * D_RWKV, 3 * D_RWKV)
    lora = shifted(3 * D_RWKV, D_SHIFT)
    r, kmod, v, kap, b, lw, g = _rwkv_prep(xs_r, xs_k, xs_v, lora, wl_ref, w0_ref, a0_ref, kk_ref, ka_ref,
                                           mhead_ref)
    r_ref[...] = r
    k_ref[...] = kmod
    v_ref[...] = v
    kap_ref[...] = kap
    b_ref[...] = b
    w_ref[...] = jnp.exp(lw)
    g_ref[...] = g

    p_in = jnp.dot(hb, win_ref[:, D_SHIFT:D_SHIFT + D_POOL], preferred_element_type=F32)
    pooled = []
    for gi, win in enumerate(POOL_WINDOWS):
        sl = slice(gi * POOL_GROUP, (gi + 1) * POOL_GROUP)
        acc = p_in[:, sl]
        for d in range(1, win):
            acc = acc + pst_ref[POOL_BUF - d, :, sl]
        pooled.append(acc / float(win))
    gates = jnp.dot(hb, win_ref[:, D_SHIFT + D_POOL:IN_COLS], preferred_element_type=F32)
    gpool = _pool_tail(pooled, p_in, gates[:, 0:D_MODEL], poolw_ref, pscale_ref, wbp_ref)
    gpool_ref[...] = gpool.astype(BF16)
    sgr_ref[...] = jax.nn.sigmoid(gates[:, D_MODEL:2 * D_MODEL]).astype(BF16)
    for i in range(POOL_BUF - 1):
        pool_ref[i] = pst_ref[i + 1]
    pool_ref[POOL_BUF - 1] = p_in


def _mix_in_weight_specs():
    return [_const_spec((D_MODEL, IN_COLS)),
            _const_spec((1, D_SHIFT)),
            _const_spec((D_LORA, 3 * D_RWKV)),
            _const_spec((1, D_RWKV)),
            _const_spec((1, D_RWKV)),
            _const_spec((1, D_RWKV)),
            _const_spec((1, D_RWKV)),
            _const_spec((D_RWKV, D_RWKV)),
            _const_spec((len(POOL_WINDOWS), POOL_GROUP, POOL_GROUP)),
            _const_spec((1, D_POOL)),
            _const_spec((D_POOL, D_MODEL))]


def _mix_in_prompt(x2d, mod3, wts, batch, seq):
    tm = min(ROW_TILE, seq)
    nj = seq // tm
    rows = batch * seq
    row_spec = pl.BlockSpec((tm, D_MODEL), lambda b, j: (b * nj + j, 0))
    f32_rows = jax.ShapeDtypeStruct((rows, D_MODEL), F32)
    bf_rows = jax.ShapeDtypeStruct((rows, D_MODEL), BF16)
    return pl.pallas_call(
        functools.partial(_mix_in_prompt_kernel, tm=tm),
        grid=(batch, nj),
        in_specs=[row_spec, pl.BlockSpec((None, 1, 6 * D_MODEL), lambda b, j: (b, 0, 0))] + _mix_in_weight_specs(),
        out_specs=[row_spec] * 9 + [pl.BlockSpec((None, 1, D_MODEL), lambda b, j: (b, 0, 0)),
                                    pl.BlockSpec((None, 16, D_POOL), lambda b, j: (b, 0, 0))],
        out_shape=[f32_rows] * 6 + [bf_rows] * 3 + [jax.ShapeDtypeStruct((batch, 1, D_MODEL), F32),
                                                    jax.ShapeDtypeStruct((batch, 16, D_POOL), F32)],
        scratch_shapes=[pltpu.VMEM((8, D_SHIFT), F32), pltpu.VMEM((tm + 16, D_POOL), F32)],
        compiler_params=_params(("arbitrary", "arbitrary")),
    )(x2d, mod3, *wts)


def _mix_in_sample(x2d, mod2, sprev, pool_t, wts):
    n = x2d.shape[0]
    full = lambda shape: pl.BlockSpec(shape, lambda i: (0,) * len(shape), pipeline_mode=pl.Buffered(1))
    out_full = lambda shape: pl.BlockSpec(shape, lambda i: (0,) * len(shape))
    f32_rows = jax.ShapeDtypeStruct((n, D_MODEL), F32)
    bf_rows = jax.ShapeDtypeStruct((n, D_MODEL), BF16)
    return pl.pallas_call(
        _mix_in_sample_kernel,
        grid=(1,),
        in_specs=[full((n, D_MODEL)), full((n, 6 * D_MODEL)), full((n, D_MODEL)), full((POOL_BUF, n, D_POOL))]
        + _mix_in_weight_specs(),
        out_specs=[out_full((n, D_MODEL))] * 10 + [out_full((POOL_BUF, n, D_POOL))],
        out_shape=[f32_rows] * 7 + [bf_rows] * 2 + [f32_rows, jax.ShapeDtypeStruct((POOL_BUF, n, D_POOL), F32)],
        compiler_params=_params(("arbitrary",)),
    )(x2d, mod2, sprev, pool_t, *wts)


def _head_epilogue(y, r, kmod, v, g, rk, gnw, gnb):
    mu = jnp.mean(y, axis=-1, keepdims=True)
    yc = y - mu
    var = jnp.mean(yc * yc, axis=-1, keepdims=True)
    yn = yc * lax.rsqrt(var + GN_EPS) * gnw + gnb
    bonus = jnp.sum(r * kmod * rk, axis=-1, keepdims=True) * v
    return (yn + bonus) * g


def _wkv_prompt_kernel(r_ref, k_ref, v_ref, kap_ref, b_ref, lw_ref, g_ref, rk_ref, gnw_ref, gnb_ref, tri_ref,
                       yg_ref, snew_ref, s_ref):
    c = pl.program_id(1)

    @pl.when(c == 0)
    def _():
        s_ref[...] = jnp.zeros(s_ref.shape, F32)

    L = CHUNK
    lw = lw_ref[...]
    p_inc = jnp.dot(tri_ref[...], lw, precision=lax.Precision.HIGHEST, preferred_element_type=F32)
    p_last = p_inc[L - 1:L, :]
    e_p = jnp.exp(p_inc)
    e_np = jnp.exp(-p_inc)
    e_e = jnp.exp(p_inc - lw)
    e_l = jnp.exp(p_last - p_inc)
    dec_last = jnp.exp(p_last)
    r = r_ref[...]
    kmod = k_ref[...]
    v = v_ref[...]
    b = b_ref[...]
    g = g_ref[...].astype(F32)
    r_hat = r * e_p
    a_hat = -kap_ref[...] * e_e
    b_hat = b * e_np
    k_hat = kmod * e_np
    b_til = b * e_l
    k_til = kmod * e_l

    row = lax.broadcasted_iota(jnp.int32, (L, 2 * L), 0)
    col = lax.broadcasted_iota(jnp.int32, (L, 2 * L), 1)
    src = jnp.where(col < L, col, col - L)
    strict = src < row
    incl = src <= row
    right = col >= L
    eye = (lax.broadcasted_iota(jnp.int32, (L, L), 0) == lax.broadcasted_iota(jnp.int32, (L, L), 1)).astype(F32)

    heads = range(N_HEADS)
    sls = [slice(hd * HEAD_SIZE, (hd + 1) * HEAD_SIZE) for hd in heads]
    ar = [jnp.concatenate([a_hat[:, sl], r_hat[:, sl]], axis=0).astype(BF16) for sl in sls]
    bk = [jnp.concatenate([b_hat[:, sl], k_hat[:, sl]], axis=0).astype(BF16) for sl in sls]
    gram = [_dot_nt(ar[h], bk[h]) for h in heads]
    g_top = [jnp.where(strict, gram[h][0:L], 0.0) for h in heads]
    g_bot = [jnp.where(incl, gram[h][L:2 * L], 0.0).astype(BF16) for h in heads]
    pw = [g_top[h][:, 0:L] for h in heads]
    inv = [eye + pw[h] for h in heads]
    for _ in range(5):
        pw = [_dot(pw[h], pw[h]) for h in heads]
        inv = [inv[h] + _dot(inv[h], pw[h]) for h in heads]
    vh = [v[:, sl] for sl in sls]
    akv = [_dot(jnp.where(right, g_top[h], 0.0), jnp.concatenate([vh[h], vh[h]], axis=0)) for h in heads]
    s_in = [s_ref[h] for h in heads]
    ars = [_dot_nt(ar[h], s_in[h]) for h in heads]
    u = [_dot(inv[h], ars[h][0:L] + akv[h]) for h in heads]
    uv = [jnp.concatenate([u[h], vh[h]], axis=0).astype(BF16) for h in heads]
    y = [ars[h][L:2 * L] + _dot(g_bot[h], uv[h]) for h in heads]
    for h, sl in enumerate(sls):
        s_new = s_in[h] * dec_last[:, sl] + _dot_tn(uv[h], jnp.concatenate([b_til[:, sl], k_til[:, sl]], axis=0))
        s_ref[h] = s_new
        snew_ref[h] = s_new
    for h, sl in enumerate(sls):
        out = _head_epilogue(y[h], r[:, sl], kmod[:, sl], vh[h], g[:, sl], rk_ref[:, sl], gnw_ref[:, sl],
                             gnb_ref[:, sl])
        yg_ref[:, sl] = out.astype(BF16)


def _wkv_prompt(ops, rk, gnw, gnb, batch, seq):
    nc = seq // CHUNK
    rows = batch * seq
    row_spec = pl.BlockSpec((CHUNK, D_RWKV), lambda b, c: (b * nc + c, 0))
    tri = jnp.tril(jnp.ones((CHUNK, CHUNK), F32))
    return pl.pallas_call(
        _wkv_prompt_kernel,
        grid=(batch, nc),
        in_specs=[row_spec] * 7 + [_const_spec((1, D_RWKV))] * 3 + [_const_spec((CHUNK, CHUNK))],
        out_specs=[row_spec, pl.BlockSpec((None, N_HEADS, HEAD_SIZE, HEAD_SIZE), lambda b, c: (b, 0, 0, 0))],
        out_shape=[jax.ShapeDtypeStruct((rows, D_RWKV), BF16),
                   jax.ShapeDtypeStruct((batch, N_HEADS, HEAD_SIZE, HEAD_SIZE), F32)],
        scratch_shapes=[pltpu.VMEM((N_HEADS, HEAD_SIZE, HEAD_SIZE), F32)],
        compiler_params=_params(("arbitrary", "arbitrary")),
    )(*ops, rk, gnw, gnb, tri)


def _wkv_sample_kernel(r_ref, k_ref, v_ref, kap_ref, b_ref, w_ref, g_ref, rk_ref, gnw_ref, gnb_ref, mhead_ref, s_ref,
                       yg_ref, snew_ref, y_ref):
    n = HEAD_SIZE
    heads = range(N_HEADS)
    sls = [slice(hd * n, (hd + 1) * n) for hd in heads]
    pad = jnp.zeros((6, n), F32)

    def body(i, carry):
        rows = pl.ds(i, 1)
        r_row, k_row, v_row, kap_row = r_ref[rows, :], k_ref[rows, :], v_ref[rows, :], kap_ref[rows, :]
        nb_row, w_row = -b_ref[rows, :], w_ref[rows, :]
        s_in = [s_ref[i, h] for h in heads]
        sa = [_dot_nt(kap_row[:, sl], s_in[h]) for h, sl in enumerate(sls)]
        upd = [_dot_tn(jnp.concatenate([sa[h], v_row[:, sl], pad], axis=0),
                       jnp.concatenate([nb_row[:, sl], k_row[:, sl], pad], axis=0)) for h, sl in enumerate(sls)]
        s_new = [s_in[h] * w_row[:, sl] + upd[h] for h, sl in enumerate(sls)]
        for h in heads:
            snew_ref[i, h] = s_new[h]
        y = [_dot_nt(r_row[:, sl], s_new[h]) for h, sl in enumerate(sls)]
        y_ref[rows, :] = jnp.concatenate(y, axis=-1)
        return carry

    lax.fori_loop(0, SAMPLE_TILE, body, 0)

    inv_n = 1.0 / n
    y = y_ref[...]
    r, kmod, v = r_ref[...], k_ref[...], v_ref[...]
    mu = _dot(y, mhead_ref[...]) * inv_n
    yc = y - mu
    var = _dot(yc * yc, mhead_ref[...]) * inv_n
    yn = yc * lax.rsqrt(var + GN_EPS) * gnw_ref[...] + gnb_ref[...]
    bonus = _dot(r * kmod * rk_ref[...], mhead_ref[...]) * v
    yg_ref[...] = (yn + bonus) * g_ref[...]


def _wkv_sample(ops, rk, gnw, gnb, mhead, state):
    n = state.shape[0]
    bt = min(SAMPLE_TILE, n)
    row_spec = pl.BlockSpec((bt, D_RWKV), lambda i: (i, 0))
    st_spec = pl.BlockSpec((bt, N_HEADS, HEAD_SIZE, HEAD_SIZE), lambda i: (i, 0, 0, 0))
    return pl.pallas_call(
        _wkv_sample_kernel,
        grid=(n // bt,),
        in_specs=[row_spec] * 7 + [_const_spec((1, D_RWKV))] * 3 + [_const_spec((D_RWKV, D_RWKV)), st_spec],
        out_specs=[row_spec, st_spec],
        out_shape=[jax.ShapeDtypeStruct((n, D_RWKV), F32), jax.ShapeDtypeStruct(state.shape, F32)],
        scratch_shapes=[pltpu.VMEM((bt, D_RWKV), F32)],
        compiler_params=_params(("arbitrary",)),
    )(*ops, rk, gnw, gnb, mhead, state)


def _mix_out_kernel(yg_ref, gpool_ref, sgr_ref, x_ref, mod_ref, wbr_ref, wout_ref, lnw_ref, lnb_ref, o_ref, *, alpha):
    y_rwkv = _dot(yg_ref[...], wbr_ref[...])
    merged = gpool_ref[...].astype(F32) + sgr_ref[...].astype(F32) * y_rwkv
    mix = _dot(merged, wout_ref[...])
    g1 = mod_ref[:, 2 * D_MODEL:3 * D_MODEL]
    o_ref[...] = _ln(alpha * x_ref[...] + (1.0 + g1) * mix) * lnw_ref[...] + lnb_ref[...]


def _mix_out(yg, gpool, sgr, x2d, mod, mod_spec, grid, row_map, tm, wts, alpha):
    row_spec = pl.BlockSpec((tm, D_MODEL), row_map)
    return pl.pallas_call(
        functools.partial(_mix_out_kernel, alpha=alpha),
        grid=grid,
        in_specs=[row_spec] * 4 + [mod_spec, _const_spec((D_RWKV, D_MODEL)), _const_spec((D_MODEL, D_MODEL)),
                                  _const_spec((1, D_MODEL)), _const_spec((1, D_MODEL))],
        out_specs=row_spec,
        out_shape=jax.ShapeDtypeStruct(x2d.shape, F32),
        compiler_params=_params(("arbitrary",) * len(grid)),
    )(yg, gpool, sgr, x2d, mod, *wts)


def _ffn_prompt_kernel(x_ref, mod_ref, up_ref, cw_ref, cb_ref, down_ref, lnw_ref, lnb_ref,
                       o_ref, conv_ref, carry_ref, *, tm, alpha):
    j = pl.program_id(1)
    first = j == 0
    x = x_ref[...]
    h2 = _ln(x) * (1.0 + mod_ref[:, 4 * D_MODEL:5 * D_MODEL]) + mod_ref[:, 3 * D_MODEL:4 * D_MODEL]
    hb = h2.astype(BF16)
    row = lax.broadcasted_iota(jnp.int32, (tm, 1), 0)

    @pl.when(first)
    def _():
        carry_ref[...] = jnp.zeros(carry_ref.shape, F32)

    def conv(c0, c1):
        u = jnp.dot(hb, up_ref[:, c0:c1], preferred_element_type=F32)
        c6 = carry_ref[6:7, c0:c1]
        c7 = carry_ref[7:8, c0:c1]
        p1 = jnp.where(row == 0, c7, pltpu.roll(u, 1, 0))
        p2 = jnp.where(row == 0, c6, jnp.where(row == 1, c7, pltpu.roll(u, 2, 0)))
        carry_ref[6:8, c0:c1] = u[tm - 2:tm, :]
        conv_ref[:, c0:c1] = u[tm - 2:tm, :]
        return cb_ref[:, c0:c1] + p2 * cw_ref[0:1, c0:c1] + p1 * cw_ref[1:2, c0:c1] + u * cw_ref[2:3, c0:c1]

    ff = jnp.zeros((tm, D_MODEL), F32)
    for c in range(D_FF // FF_CHUNK):
        val = conv(c * FF_CHUNK, (c + 1) * FF_CHUNK)
        gate = conv(D_FF + c * FF_CHUNK, D_FF + (c + 1) * FF_CHUNK)
        ff = ff + _dot(val * jax.nn.silu(gate), down_ref[c * FF_CHUNK:(c + 1) * FF_CHUNK, :])
    g2 = mod_ref[:, 5 * D_MODEL:6 * D_MODEL]
    o_ref[...] = _ln(alpha * x + (1.0 + g2) * ff) * lnw_ref[...] + lnb_ref[...]


def _ffn_weight_specs():
    return [_const_spec((D_MODEL, 2 * D_FF)), _const_spec((CONV_W, 2 * D_FF)), _const_spec((1, 2 * D_FF)),
            _const_spec((D_FF, D_MODEL)), _const_spec((1, D_MODEL)), _const_spec((1, D_MODEL))]


def _ffn_prompt(x2d, mod3, wts, batch, seq, alpha):
    tm = min(ROW_TILE, seq)
    nj = seq // tm
    row_spec = pl.BlockSpec((tm, D_MODEL), lambda b, j: (b * nj + j, 0))
    return pl.pallas_call(
        functools.partial(_ffn_prompt_kernel, tm=tm, alpha=alpha),
        grid=(batch, nj),
        in_specs=[row_spec, pl.BlockSpec((None, 1, 6 * D_MODEL), lambda b, j: (b, 0, 0))] + _ffn_weight_specs(),
        out_specs=[row_spec, pl.BlockSpec((None, CONV_W - 1, 2 * D_FF), lambda b, j: (b, 0, 0))],
        out_shape=[jax.ShapeDtypeStruct(x2d.shape, F32),
                   jax.ShapeDtypeStruct((batch, CONV_W - 1, 2 * D_FF), F32)],
        scratch_shapes=[pltpu.VMEM((8, 2 * D_FF), F32)],
        compiler_params=_params(("arbitrary", "arbitrary")),
    )(x2d, mod3, *wts)


def _ffn_sample_kernel(x_ref, mod_ref, cst_ref, up_ref, cw_ref, cb_ref, down_ref, lnw_ref, lnb_ref,
                       o_ref, conv_ref, *, alpha):
    x = x_ref[...]
    h2 = _ln(x) * (1.0 + mod_ref[:, 4 * D_MODEL:5 * D_MODEL]) + mod_ref[:, 3 * D_MODEL:4 * D_MODEL]
    hb = h2.astype(BF16)

    def conv(c0, c1):
        u = jnp.dot(hb, up_ref[:, c0:c1], preferred_element_type=F32)
        p2 = cst_ref[0, :, c0:c1]
        p1 = cst_ref[1, :, c0:c1]
        conv_ref[0, :, c0:c1] = p1
        conv_ref[1, :, c0:c1] = u
        return cb_ref[:, c0:c1] + p2 * cw_ref[0:1, c0:c1] + p1 * cw_ref[1:2, c0:c1] + u * cw_ref[2:3, c0:c1]

    ff = jnp.zeros(x.shape, F32)
    for c in range(D_FF // FF_CHUNK):
        val = conv(c * FF_CHUNK, (c + 1) * FF_CHUNK)
        gate = conv(D_FF + c * FF_CHUNK, D_FF + (c + 1) * FF_CHUNK)
        ff = ff + _dot(val * jax.nn.silu(gate), down_ref[c * FF_CHUNK:(c + 1) * FF_CHUNK, :])
    g2 = mod_ref[:, 5 * D_MODEL:6 * D_MODEL]
    o_ref[...] = _ln(alpha * x + (1.0 + g2) * ff) * lnw_ref[...] + lnb_ref[...]


def _ffn_sample(x2d, mod2, conv_t, wts, alpha):
    n = x2d.shape[0]
    full = lambda shape: pl.BlockSpec(shape, lambda i: (0,) * len(shape), pipeline_mode=pl.Buffered(1))
    out_full = lambda shape: pl.BlockSpec(shape, lambda i: (0,) * len(shape))
    return pl.pallas_call(
        functools.partial(_ffn_sample_kernel, alpha=alpha),
        grid=(1,),
        in_specs=[full((n, D_MODEL)), full((n, 6 * D_MODEL)), full((CONV_W - 1, n, 2 * D_FF))] + _ffn_weight_specs(),
        out_specs=[out_full((n, D_MODEL)), out_full((CONV_W - 1, n, 2 * D_FF))],
        out_shape=[jax.ShapeDtypeStruct(x2d.shape, F32), jax.ShapeDtypeStruct((CONV_W - 1, n, 2 * D_FF), F32)],
        compiler_params=_params(("arbitrary",)),
    )(x2d, mod2, conv_t, *wts)


def _layer_weights(l, p):
    row = lambda a: a[l].reshape(1, -1)
    bf = lambda a: a[l].astype(BF16)
    lora_up = jnp.zeros((D_LORA, 3 * D_RWKV), F32)
    lora_up = lora_up.at[0:D_DECAY_LORA, 0:D_RWKV].set(p['decay_up'][l])
    lora_up = lora_up.at[D_DECAY_LORA:D_DECAY_LORA + D_AAA_LORA, D_RWKV:2 * D_RWKV].set(p['aaa_up'][l])
    lora_up = lora_up.at[D_DECAY_LORA + D_AAA_LORA:, 2 * D_RWKV:].set(p['gate_up'][l])
    head = jnp.arange(D_RWKV) // HEAD_SIZE
    mhead = (head[:, None] == head[None, :]).astype(BF16)
    mix_in = (bf(p['w_in']), row(p['mu_shift']), lora_up.astype(BF16), row(p['decay_w0']), row(p['aaa_a0']),
              row(p['k_k']), row(p['k_a']), mhead, bf(p['pool_w']), row(p['pool_scale']), bf(p['w_branch_pool']))
    wkv = (row(p['r_k']), row(p['gn_w']), row(p['gn_b']))
    mix_out = (bf(p['w_branch_rwkv']), bf(p['w_out']), row(p['ln1_w']), row(p['ln1_b']))
    ffn = (bf(p['ffn_up']), p['conv_w'][l], row(p['conv_b']), bf(p['ffn_down']), row(p['ln2_w']), row(p['ln2_b']))
    return mix_in, wkv, mix_out, ffn


def _trunk(x_prompt, x_sample, state_shift, state_wkv, state_pool, state_conv, mod, p):
    depth = p['w_in'].shape[0]
    alpha = float((2 * depth) ** 0.25)
    batch, seq, _ = x_prompt.shape
    nsamp = x_sample.shape[0]
    xp = x_prompt.reshape(batch * seq, D_MODEL).astype(F32)
    xs = x_sample.reshape(nsamp, D_MODEL).astype(F32)
    tm = min(ROW_TILE, seq)
    nj = seq // tm
    prompt_mod_spec = pl.BlockSpec((None, 1, 6 * D_MODEL), lambda i: (i // nj, 0, 0))
    sample_mod_spec = pl.BlockSpec((nsamp, 6 * D_MODEL), lambda i: (0, 0))
    outs = {k: [] for k in ('shift_p', 'wkv_p', 'pool_p', 'conv_p', 'shift_s', 'wkv_s', 'pool_s', 'conv_s')}
    for l in range(depth):
        w_mix_in, w_wkv, w_mix_out, w_ffn = _layer_weights(l, p)
        mod_p = mod[l, 0:batch].reshape(batch, 1, 6 * D_MODEL)
        mod_s = mod[l, batch:batch + nsamp]

        *ops, gpool, sgr, shift_p, pool_p = _mix_in_prompt(xp, mod_p, w_mix_in, batch, seq)
        yg, wkv_p = _wkv_prompt(ops, *w_wkv, batch, seq)
        xp = _mix_out(yg, gpool, sgr, xp, mod_p, prompt_mod_spec, (batch * nj,), lambda i: (i, 0), tm,
                      w_mix_out, alpha)
        xp, conv_p = _ffn_prompt(xp, mod_p, w_ffn, batch, seq, alpha)
        outs['shift_p'].append(shift_p.reshape(batch, D_MODEL))
        outs['wkv_p'].append(wkv_p)
        outs['pool_p'].append(pool_p[:, 1:, :])
        outs['conv_p'].append(conv_p)

        pool_t = jnp.swapaxes(state_pool[l].astype(F32), 0, 1)
        conv_t = jnp.swapaxes(state_conv[l].astype(F32), 0, 1)
        *ops, gpool, sgr, shift_s, pool_s = _mix_in_sample(xs, mod_s, state_shift[l].astype(F32), pool_t, w_mix_in)
        yg, wkv_s = _wkv_sample(ops, *w_wkv, w_mix_in[7], state_wkv[l].astype(F32))
        xs = _mix_out(yg, gpool, sgr, xs, mod_s, sample_mod_spec, (1,), lambda i: (0, 0), nsamp, w_mix_out, alpha)
        xs, conv_s = _ffn_sample(xs, mod_s, conv_t, w_ffn, alpha)
        outs['shift_s'].append(shift_s)
        outs['wkv_s'].append(wkv_s)
        outs['pool_s'].append(jnp.swapaxes(pool_s, 0, 1))
        outs['conv_s'].append(jnp.swapaxes(conv_s, 0, 1))
    st = {k: jnp.stack(v) for k, v in outs.items()}
    return (xp.reshape(batch, seq, D_MODEL), xs.reshape(nsamp, 1, D_MODEL), st['shift_p'], st['wkv_p'], st['pool_p'],
            st['conv_p'], st['shift_s'], st['wkv_s'], st['pool_s'], st['conv_s'])


def kernel(x_prompt, x_sample, state_shift, state_wkv, state_pool, state_conv, c_prompt, c_sample, ada_w, ada_b, w_in, mu_shift, decay_w0, decay_up, aaa_a0, aaa_up, gate_up, k_k, k_a, r_k, gn_w, gn_b, w_branch_rwkv, pool_w, pool_scale, w_branch_pool, w_out, ln1_w, ln1_b, ffn_up, conv_w, conv_b, ffn_down, ln2_w, ln2_b):
    p = {'w_in': w_in, 'mu_shift': mu_shift, 'decay_w0': decay_w0, 'decay_up': decay_up, 'aaa_a0': aaa_a0,
         'aaa_up': aaa_up, 'gate_up': gate_up, 'k_k': k_k, 'k_a': k_a, 'r_k': r_k, 'gn_w': gn_w, 'gn_b': gn_b,
         'w_branch_rwkv': w_branch_rwkv, 'pool_w': pool_w, 'pool_scale': pool_scale,
         'w_branch_pool': w_branch_pool, 'w_out': w_out, 'ln1_w': ln1_w, 'ln1_b': ln1_b, 'ffn_up': ffn_up,
         'conv_w': conv_w, 'conv_b': conv_b, 'ffn_down': ffn_down, 'ln2_w': ln2_w, 'ln2_b': ln2_b}
    c_all = jnp.concatenate([c_prompt, c_sample], axis=0).astype(F32)
    mod = _modulation(c_all, ada_w, ada_b)
    dt = x_prompt.dtype
    out = _trunk(x_prompt, x_sample, state_shift, state_wkv, state_pool, state_conv, mod, p)
    return tuple(o.astype(dt) for o in out)
```

```python
import functools
import math

import jax
import jax.numpy as jnp
from jax import lax
from jax.experimental import pallas as pl
from jax.experimental.pallas import tpu as pltpu

F32 = jnp.float32
BF16 = jnp.bfloat16

D_MODEL = 1024
HEAD_SIZE = 64
N_HEADS = D_MODEL // HEAD_SIZE
D_RWKV = N_HEADS * HEAD_SIZE
D_DECAY_LORA = 64
D_AAA_LORA = 64
D_GATE_LORA = 128
D_LORA = D_DECAY_LORA + D_AAA_LORA + D_GATE_LORA
POOL_WINDOWS = (2, 4, 8, 16)
POOL_GROUP = D_MODEL // 8
D_POOL = len(POOL_WINDOWS) * POOL_GROUP
POOL_BUF = max(POOL_WINDOWS) - 1
D_FF = (D_MODEL * 11) // 4
CONV_W = 3
D_SHIFT = 3 * D_RWKV + D_LORA
IN_COLS = D_SHIFT + D_POOL + 2 * D_MODEL
LN_EPS = 1e-5
GN_EPS = 64e-5
EXP_M05 = math.exp(-0.5)

VMEM_LIMIT_BYTES = 56 * 1024 * 1024
ROW_TILE = 256
CHUNK = 64
WKV_CHUNKS_PER_STEP = 4
SAMPLE_TILE = 16
FF_CHUNK = 1408


def _ln(x):
    mu = jnp.mean(x, axis=-1, keepdims=True)
    xc = x - mu
    var = jnp.mean(xc * xc, axis=-1, keepdims=True)
    return xc * lax.rsqrt(var + LN_EPS)


def _dot(a, b):
    return jnp.dot(a.astype(BF16), b.astype(BF16), preferred_element_type=F32)


def _dot_nt(a, b):
    return lax.dot_general(a.astype(BF16), b.astype(BF16), (((1,), (1,)), ((), ())),
                           preferred_element_type=F32)


def _dot_tn(a, b):
    return lax.dot_general(a.astype(BF16), b.astype(BF16), (((0,), (0,)), ((), ())),
                           preferred_element_type=F32)


def _const_spec(shape):
    nd = len(shape)
    return pl.BlockSpec(shape, lambda *_: (0,) * nd, pipeline_mode=pl.Buffered(1))


def _layer_spec(shape, l):
    nd = len(shape)
    return pl.BlockSpec((None,) + tuple(shape), lambda *_: (l,) + (0,) * nd, pipeline_mode=pl.Buffered(1))


def _params(sem):
    return pltpu.CompilerParams(dimension_semantics=sem, vmem_limit_bytes=VMEM_LIMIT_BYTES)


def _mod_kernel(c_ref, w_ref, b_ref, o_ref):
    o_ref[...] = _dot(c_ref[...], w_ref[...]) + b_ref[...]


def _modulation(c_all, ada_w, ada_b):
    depth, d, n = ada_w.shape
    rows = c_all.shape[0]
    tn = 1536
    return pl.pallas_call(
        _mod_kernel,
        grid=(depth, n // tn),
        in_specs=[pl.BlockSpec((rows, d), lambda l, j: (0, 0)),
                  pl.BlockSpec((None, d, tn), lambda l, j: (l, 0, j)),
                  pl.BlockSpec((None, 1, tn), lambda l, j: (l, 0, j))],
        out_specs=pl.BlockSpec((None, rows, tn), lambda l, j: (l, 0, j)),
        out_shape=jax.ShapeDtypeStruct((depth, rows, n), F32),
        compiler_params=_params(("arbitrary", "arbitrary")),
    )(c_all, ada_w, ada_b.reshape(depth, 1, n))


def _rwkv_prep(xs_r, xs_k, xs_v, lora, wl_ref, w0_ref, a0_ref, kk_ref, ka_ref, mhead_ref):
    lane = lax.broadcasted_iota(jnp.int32, lora.shape, 1)
    act = jnp.where(lane < D_DECAY_LORA, jnp.tanh(lora),
                    jnp.where(lane < D_DECAY_LORA + D_AAA_LORA, lora, jax.nn.sigmoid(lora)))
    z = _dot(act, wl_ref[...])
    dec = w0_ref[...] + z[:, 0:D_RWKV]
    lw = -jax.nn.sigmoid(dec) * EXP_M05
    a = jax.nn.sigmoid(a0_ref[...] + z[:, D_RWKV:2 * D_RWKV])
    g = z[:, 2 * D_RWKV:3 * D_RWKV]
    kk = xs_k * kk_ref[...]
    ss = _dot(kk * kk, mhead_ref[...])
    kap = kk / jnp.maximum(jnp.sqrt(ss), 1e-12)
    kmod = xs_k * (1.0 + (a - 1.0) * ka_ref[...])
    return xs_r, kmod, xs_v, kap, kap * a, lw, g


def _pool_tail(pooled, p_in, gate_pool, poolw_ref, pscale_ref, wbp_ref):
    mixed = []
    for gi in range(len(POOL_WINDOWS)):
        sl = slice(gi * POOL_GROUP, (gi + 1) * POOL_GROUP)
        diff = pooled[gi] - p_in[:, sl]
        mixed.append(_dot(diff, poolw_ref[gi]) * pscale_ref[:, sl])
    y_pool = _dot(jnp.concatenate(mixed, axis=-1), wbp_ref[...])
    return jax.nn.sigmoid(gate_pool) * y_pool


def _mix_in_prompt_kernel(x_ref, mod_ref, win_ref, mu_ref, wl_ref, w0_ref, a0_ref, kk_ref, ka_ref,
                          mhead_ref, poolw_ref, pscale_ref, wbp_ref,
                          r_ref, k_ref, v_ref, kap_ref, b_ref, lw_ref, g_ref, gpool_ref, sgr_ref,
                          shift_ref, pool_ref,
                          carry_ref, pext_ref, *, tm):
    j = pl.program_id(1)
    first = j == 0
    x = x_ref[...]
    h = _ln(x) * (1.0 + mod_ref[:, D_MODEL:2 * D_MODEL]) + mod_ref[:, 0:D_MODEL]
    shift_ref[...] = h[tm - 1:tm, :]
    hb = h.astype(BF16)
    row = lax.broadcasted_iota(jnp.int32, (tm, 1), 0)

    @pl.when(first)
    def _():
        carry_ref[...] = jnp.zeros(carry_ref.shape, F32)

    def shifted(c0, c1):
        cur = jnp.dot(hb, win_ref[:, c0:c1], preferred_element_type=F32)
        prev = jnp.where(row == 0, carry_ref[7:8, c0:c1], pltpu.roll(cur, 1, 0))
        carry_ref[7:8, c0:c1] = cur[tm - 1:tm, :]
        return cur + (prev - cur) * mu_ref[:, c0:c1]

    xs_r = shifted(0, D_RWKV)
    xs_k = shifted(D_RWKV, 2 * D_RWKV)
    xs_v = shifted(2 * D_RWKV, 3 * D_RWKV)
    lora = shifted(3 * D_RWKV, D_SHIFT)
    r, kmod, v, kap, b, lw, g = _rwkv_prep(xs_r, xs_k, xs_v, lora, wl_ref, w0_ref, a0_ref, kk_ref, ka_ref,
                                           mhead_ref)
    r_ref[...] = r
    k_ref[...] = kmod
    v_ref[...] = v
    kap_ref[...] = kap
    b_ref[...] = b
    lw_ref[...] = lw
    g_ref[...] = g.astype(BF16)

    p_in = jnp.dot(hb, win_ref[:, D_SHIFT:D_SHIFT + D_POOL], preferred_element_type=F32)

    @pl.when(first)
    def _():
        pext_ref[0:16, :] = jnp.zeros((16, D_POOL), F32)

    @pl.when(jnp.logical_not(first))
    def _():
        pext_ref[0:16, :] = pext_ref[tm:tm + 16, :]

    pext_ref[16:16 + tm, :] = p_in
    pos1 = (j * tm + row + 1).astype(F32)
    pooled = []
    for gi, win in enumerate(POOL_WINDOWS):
        sl = slice(gi * POOL_GROUP, (gi + 1) * POOL_GROUP)
        acc = p_in[:, sl]
        for d in range(1, win):
            acc = acc + pext_ref[16 - d:16 - d + tm, sl]
        pooled.append(acc / jnp.minimum(float(win), pos1))
    gates = jnp.dot(hb, win_ref[:, D_SHIFT + D_POOL:IN_COLS], preferred_element_type=F32)
    gpool = _pool_tail(pooled, p_in, gates[:, 0:D_MODEL], poolw_ref, pscale_ref, wbp_ref)
    gpool_ref[...] = gpool.astype(BF16)
    sgr_ref[...] = jax.nn.sigmoid(gates[:, D_MODEL:2 * D_MODEL]).astype(BF16)
    pool_ref[...] = pext_ref[tm:tm + 16, :]


def _mix_in_sample_kernel(x_ref, mod_ref, sprev_ref, pst_ref, win_ref, mu_ref, wl_ref, w0_ref, a0_ref, kk_ref,
                          ka_ref, mhead_ref, poolw_ref, pscale_ref, wbp_ref,
                          r_ref, k_ref, v_ref, kap_ref, b_ref, w_ref, g_ref, gpool_ref, sgr_ref,
                          shift_ref, pool_ref):
    x = x_ref[...]
    h = _ln(x) * (1.0 + mod_ref[:, D_MODEL:2 * D_MODEL]) + mod_ref[:, 0:D_MODEL]
    shift_ref[...] = h
    hb = h.astype(BF16)
    pb = sprev_ref[...].astype(BF16)

    def shifted(c0, c1):
        w = win_ref[:, c0:c1]
        cur = jnp.dot(hb, w, preferred_element_type=F32)
        prev = jnp.dot(pb, w, preferred_element_type=F32)
        return cur + (prev - cur) * mu_ref[:, c0:c1]

    xs_r = shifted(0, D_RWKV)
    xs_k = shifted(D_RWKV, 2 * D_RWKV)
    xs_v = shifted(2 * D_RWKV, 3 * D_RWKV)
    lora = shifted(3 * D_RWKV, D_SHIFT)
    r, kmod, v, kap, b, lw, g = _rwkv_prep(xs_r, xs_k, xs_v, lora, wl_ref, w0_ref, a0_ref, kk_ref, ka_ref,
                                           mhead_ref)
    r_ref[...] = r
    k_ref[...] = kmod
    v_ref[...] = v
    kap_ref[...] = kap
    b_ref[...] = b
    w_ref[...] = jnp.exp(lw)
    g_ref[...] = g

    p_in = jnp.dot(hb, win_ref[:, D_SHIFT:D_SHIFT + D_POOL], preferred_element_type=F32)
    pooled = []
    for gi, win in enumerate(POOL_WINDOWS):
        sl = slice(gi * POOL_GROUP, (gi + 1) * POOL_GROUP)
        acc = p_in[:, sl]
        for d in range(1, win):
            c0 = (POOL_BUF - d) * D_POOL + gi * POOL_GROUP
            acc = acc + pst_ref[:, c0:c0 + POOL_GROUP]
        pooled.append(acc / float(win))
    gates = jnp.dot(hb, win_ref[:, D_SHIFT + D_POOL:IN_COLS], preferred_element_type=F32)
    gpool = _pool_tail(pooled, p_in, gates[:, 0:D_MODEL], poolw_ref, pscale_ref, wbp_ref)
    gpool_ref[...] = gpool.astype(BF16)
    sgr_ref[...] = jax.nn.sigmoid(gates[:, D_MODEL:2 * D_MODEL]).astype(BF16)
    pool_ref[:, 0:(POOL_BUF - 1) * D_POOL] = pst_ref[:, D_POOL:POOL_BUF * D_POOL]
    pool_ref[:, (POOL_BUF - 1) * D_POOL:POOL_BUF * D_POOL] = p_in


def _mix_in_weight_specs(l):
    return [_layer_spec((D_MODEL, IN_COLS), l),
            _layer_spec((1, D_SHIFT), l),
            _layer_spec((D_LORA, 3 * D_RWKV), l),
            _layer_spec((1, D_RWKV), l),
            _layer_spec((1, D_RWKV), l),
            _layer_spec((1, D_RWKV), l),
            _layer_spec((1, D_RWKV), l),
            _const_spec((D_RWKV, D_RWKV)),
            _layer_spec((len(POOL_WINDOWS), POOL_GROUP, POOL_GROUP), l),
            _layer_spec((1, D_POOL), l),
            _layer_spec((D_POOL, D_MODEL), l)]


def _mix_in_prompt(x2d, mod3, wts, l, batch, seq):
    tm = min(ROW_TILE, seq)
    nj = seq // tm
    rows = batch * seq
    row_spec = pl.BlockSpec((tm, D_MODEL), lambda b, j: (b * nj + j, 0))
    f32_rows = jax.ShapeDtypeStruct((rows, D_MODEL), F32)
    bf_rows = jax.ShapeDtypeStruct((rows, D_MODEL), BF16)
    return pl.pallas_call(
        functools.partial(_mix_in_prompt_kernel, tm=tm),
        grid=(batch, nj),
        in_specs=[row_spec, pl.BlockSpec((None, 1, 6 * D_MODEL), lambda b, j: (b, 0, 0))] + _mix_in_weight_specs(l),
        out_specs=[row_spec] * 9 + [pl.BlockSpec((None, 1, D_MODEL), lambda b, j: (b, 0, 0)),
                                    pl.BlockSpec((None, 16, D_POOL), lambda b, j: (b, 0, 0))],
        out_shape=[f32_rows] * 6 + [bf_rows] * 3 + [jax.ShapeDtypeStruct((batch, 1, D_MODEL), F32),
                                                    jax.ShapeDtypeStruct((batch, 16, D_POOL), F32)],
        scratch_shapes=[pltpu.VMEM((8, D_SHIFT), F32), pltpu.VMEM((tm + 16, D_POOL), F32)],
        compiler_params=_params(("arbitrary", "arbitrary")),
    )(x2d, mod3, *wts)


def _mix_in_sample(x2d, mod2, shift_all, pool_all, wts, l):
    n = x2d.shape[0]
    out_full = lambda shape: pl.BlockSpec(shape, lambda i: (0,) * len(shape))
    f32_rows = jax.ShapeDtypeStruct((n, D_MODEL), F32)
    bf_rows = jax.ShapeDtypeStruct((n, D_MODEL), BF16)
    return pl.pallas_call(
        _mix_in_sample_kernel,
        grid=(1,),
        in_specs=[_const_spec((n, D_MODEL)), _const_spec((n, 6 * D_MODEL)), _layer_spec((n, D_MODEL), l),
                  _layer_spec((n, POOL_BUF * D_POOL), l)] + _mix_in_weight_specs(l),
        out_specs=[out_full((n, D_MODEL))] * 10 + [out_full((n, POOL_BUF * D_POOL))],
        out_shape=[f32_rows] * 7 + [bf_rows] * 2 + [f32_rows, jax.ShapeDtypeStruct((n, POOL_BUF * D_POOL), F32)],
        compiler_params=_params(("arbitrary",)),
    )(x2d, mod2, shift_all, pool_all, *wts)


def _wkv_prompt_kernel(r_ref, k_ref, v_ref, kap_ref, b_ref, lw_ref, g_ref, rk_ref, gnw_ref, gnb_ref, mpair_ref,
                       yg_ref, snew_ref, s_ref, *, nch):
    @pl.when(pl.program_id(1) == 0)
    def _():
        s_ref[...] = jnp.zeros(s_ref.shape, F32)

    L = CHUNK
    pw2 = 2 * HEAD_SIZE
    lane = lax.broadcasted_iota(jnp.int32, (L, pw2), 1)
    trow = lax.broadcasted_iota(jnp.int32, (L, pw2), 0)
    left = lane < HEAD_SIZE
    src = jnp.where(left, lane, lane - HEAD_SIZE)
    strict = src < trow
    incl = src <= trow
    eye = (src == trow).astype(F32)
    bd_mask = ((lax.broadcasted_iota(jnp.int32, (pw2, pw2), 0) < HEAD_SIZE)
               == (lax.broadcasted_iota(jnp.int32, (pw2, pw2), 1) < HEAD_SIZE))

    crow = lax.broadcasted_iota(jnp.int32, (L, 1), 0)

    def bd(x):
        xb = x.astype(BF16)
        zero = jnp.zeros_like(xb)
        return jnp.concatenate([jnp.where(left, xb, zero), jnp.where(left, zero, xb)], axis=0)

    def cumsum_rows(x):
        sh = 1
        while sh < L:
            x = x + jnp.where(crow >= sh, pltpu.roll(x, sh, 0), 0.0)
            sh *= 2
        return x

    chunks = range(nch)
    pairs = range(N_HEADS // 2)
    cs = [slice(p * pw2, (p + 1) * pw2) for p in pairs]
    units = [(j, p) for j in chunks for p in pairs]

    r, kmod, v, g, dec_last, a_hat, r_hat, b_hat, k_hat, b_til, k_til = ({} for _ in range(11))
    for j in chunks:
        rs = slice(j * L, (j + 1) * L)
        lw = lw_ref[rs, :]
        p_inc = cumsum_rows(lw)
        p_last = p_inc[L - 1:L, :]
        e_p = jnp.exp(p_inc)
        e_np = jnp.exp(-p_inc)
        e_l = jnp.exp(p_last - p_inc)
        dec_last[j] = jnp.exp(p_last)
        r[j], kmod[j], v[j], bb = r_ref[rs, :], k_ref[rs, :], v_ref[rs, :], b_ref[rs, :]
        g[j] = g_ref[rs, :].astype(F32)
        r_hat[j] = r[j] * e_p
        a_hat[j] = -kap_ref[rs, :] * jnp.exp(p_inc - lw)
        b_hat[j] = bb * e_np
        k_hat[j] = kmod[j] * e_np
        b_til[j] = bb * e_l
        k_til[j] = kmod[j] * e_l

    ar = {(j, p): jnp.concatenate([a_hat[j][:, cs[p]], r_hat[j][:, cs[p]]], axis=0).astype(BF16) for j, p in units}
    gram = {(j, p): _dot_nt(ar[j, p], jnp.concatenate([bd(b_hat[j][:, cs[p]]), bd(k_hat[j][:, cs[p]])], axis=0))
            for j, p in units}
    nab = {u_: jnp.where(strict, gram[u_][0:L, 0:pw2], 0.0) for u_ in units}
    nak = {u_: jnp.where(strict, gram[u_][0:L, pw2:2 * pw2], 0.0) for u_ in units}
    rbk = {u_: jnp.concatenate([jnp.where(incl, gram[u_][L:2 * L, 0:pw2], 0.0),
                                jnp.where(incl, gram[u_][L:2 * L, pw2:2 * pw2], 0.0)], axis=1).astype(BF16)
           for u_ in units}
    inv = {u_: eye + nab[u_] for u_ in units}
    pw = {u_: _dot(nab[u_], bd(nab[u_])) for u_ in units}
    for _ in range(4):
        px = {u_: _dot(jnp.concatenate([pw[u_], inv[u_]], axis=0), bd(pw[u_])) for u_ in units}
        pw = {u_: px[u_][0:L] for u_ in units}
        inv = {u_: inv[u_] + px[u_][L:2 * L] for u_ in units}
    inv = {u_: inv[u_] + _dot(inv[u_], bd(pw[u_])) for u_ in units}
    vp = {(j, p): v[j][:, cs[p]] for j, p in units}
    bdv = {u_: bd(vp[u_]) for u_ in units}
    akv = {u_: _dot(nak[u_], bdv[u_]) for u_ in units}

    s_cur = [s_ref[p] for p in pairs]
    y = {}
    for j in chunks:
        ars = [_dot_nt(ar[j, p], s_cur[p]) for p in pairs]
        u = [_dot(inv[j, p], bd(ars[p][0:L] + akv[j, p])) for p in pairs]
        for p in pairs:
            y[j, p] = ars[p][L:2 * L] + _dot(rbk[j, p], jnp.concatenate([bd(u[p]), bdv[j, p]], axis=0))
        upd = [_dot_tn(jnp.concatenate([u[p], vp[j, p]], axis=0),
                       jnp.concatenate([b_til[j][:, cs[p]], k_til[j][:, cs[p]]], axis=0)) for p in pairs]
        s_cur = [s_cur[p] * dec_last[j][:, cs[p]] + jnp.where(bd_mask, upd[p], 0.0) for p in pairs]
    for p in pairs:
        s_ref[p] = s_cur[p]
        snew_ref[p] = s_cur[p]

    def head_sum(x):
        st = jnp.concatenate([x[:, c_] for c_ in cs], axis=0)
        sm = _dot(st, mpair_ref[...])
        return jnp.concatenate([sm[p * L:(p + 1) * L] for p in pairs], axis=1)

    inv_n = 1.0 / HEAD_SIZE
    for j in chunks:
        y_all = jnp.concatenate([y[j, p] for p in pairs], axis=1)
        mu = head_sum(y_all) * inv_n
        yc = y_all - mu
        var = head_sum(yc * yc) * inv_n
        yn = yc * lax.rsqrt(var + GN_EPS) * gnw_ref[...] + gnb_ref[...]
        bonus = head_sum(r[j] * kmod[j] * rk_ref[...]) * v[j]
        yg_ref[j * L:(j + 1) * L, :] = ((yn + bonus) * g[j]).astype(BF16)


def _wkv_prompt(ops, rk, gnw, gnb, l, batch, seq):
    nch = math.gcd(seq // CHUNK, WKV_CHUNKS_PER_STEP)
    rows_per_step = nch * CHUNK
    nc = seq // rows_per_step
    rows = batch * seq
    npair = N_HEADS // 2
    pw2 = 2 * HEAD_SIZE
    row_spec = pl.BlockSpec((rows_per_step, D_RWKV), lambda b, c: (b * nc + c, 0))
    half = jnp.arange(pw2) // HEAD_SIZE
    mpair = (half[:, None] == half[None, :]).astype(BF16)
    yg, sbd = pl.pallas_call(
        functools.partial(_wkv_prompt_kernel, nch=nch),
        grid=(batch, nc),
        in_specs=[row_spec] * 7 + [_layer_spec((1, D_RWKV), l)] * 3 + [_const_spec((pw2, pw2))],
        out_specs=[row_spec, pl.BlockSpec((None, npair, pw2, pw2), lambda b, c: (b, 0, 0, 0))],
        out_shape=[jax.ShapeDtypeStruct((rows, D_RWKV), BF16),
                   jax.ShapeDtypeStruct((batch, npair, pw2, pw2), F32)],
        scratch_shapes=[pltpu.VMEM((npair, pw2, pw2), F32)],
        compiler_params=_params(("arbitrary", "arbitrary")),
    )(*ops, rk, gnw, gnb, mpair)
    n = HEAD_SIZE
    wkv = jnp.stack([sbd[:, :, 0:n, 0:n], sbd[:, :, n:2 * n, n:2 * n]], axis=2)
    return yg, wkv.reshape(batch, N_HEADS, n, n)


def _wkv_sample_kernel(r_ref, k_ref, v_ref, kap_ref, b_ref, w_ref, g_ref, rk_ref, gnw_ref, gnb_ref, mhead_ref, s_ref,
                       *rest, n_tiles):
    yg_ref, snew_ref, y_ref = rest[-3:]
    n = HEAD_SIZE
    heads = range(N_HEADS)
    sls = [slice(hd * n, (hd + 1) * n) for hd in heads]
    step = pl.program_id(0)

    def body(i, carry):
        rows = pl.ds(i, 1)
        pad = jnp.zeros((6, n), F32)
        r_row, k_row, v_row, kap_row = r_ref[rows, :], k_ref[rows, :], v_ref[rows, :], kap_ref[rows, :]
        nb_row, w_row = -b_ref[rows, :], w_ref[rows, :]
        s_in = [s_ref[i, h] for h in heads]
        sa = [_dot_nt(kap_row[:, sl], s_in[h]) for h, sl in enumerate(sls)]
        upd = [_dot_tn(jnp.concatenate([sa[h], v_row[:, sl], pad], axis=0),
                       jnp.concatenate([nb_row[:, sl], k_row[:, sl], pad], axis=0)) for h, sl in enumerate(sls)]
        s_new = [s_in[h] * w_row[:, sl] + upd[h] for h, sl in enumerate(sls)]
        for h in heads:
            snew_ref[i, h] = s_new[h]
        y = [_dot_nt(r_row[:, sl], s_new[h]) for h, sl in enumerate(sls)]
        y_ref[rows, :] = jnp.concatenate(y, axis=-1)
        return carry

    @pl.when(step < n_tiles)
    def _():
        lax.fori_loop(0, SAMPLE_TILE, body, 0)
        inv_n = 1.0 / n
        y = y_ref[...]
        r, kmod, v = r_ref[...], k_ref[...], v_ref[...]
        mu = _dot(y, mhead_ref[...]) * inv_n
        yc = y - mu
        var = _dot(yc * yc, mhead_ref[...]) * inv_n
        yn = yc * lax.rsqrt(var + GN_EPS) * gnw_ref[...] + gnb_ref[...]
        bonus = _dot(r * kmod * rk_ref[...], mhead_ref[...]) * v
        yg_ref[...] = (yn + bonus) * g_ref[...]

    @pl.when(step >= n_tiles)
    def _():
        snew_ref[...] = jnp.zeros(snew_ref.shape, F32)


def _wkv_sample(ops, rk, gnw, gnb, mhead, state_all, new_all, l):
    depth, n = state_all.shape[0], state_all.shape[1]
    bt = min(SAMPLE_TILE, n)
    nb = n // bt
    first = new_all is None
    assert first == (l == 0)
    tile = lambda s: jnp.minimum(s, nb - 1)
    row_spec = pl.BlockSpec((bt, D_RWKV), lambda s: (tile(s), 0))
    st_block = (None, bt, N_HEADS, HEAD_SIZE, HEAD_SIZE)
    st_in = pl.BlockSpec(st_block, lambda s: (l, tile(s), 0, 0, 0))
    st_out = pl.BlockSpec(st_block, lambda s: (l + s // nb, s % nb, 0, 0, 0))
    in_specs = [row_spec] * 7 + [_layer_spec((1, D_RWKV), l)] * 3 + [_const_spec((D_RWKV, D_RWKV)), st_in]
    args = [*ops, rk, gnw, gnb, mhead, state_all]
    aliases = {}
    if not first:
        in_specs.append(pl.BlockSpec(memory_space=pl.ANY))
        aliases = {len(args): 1}
        args.append(new_all)
    return pl.pallas_call(
        functools.partial(_wkv_sample_kernel, n_tiles=nb),
        grid=(depth * nb if first else nb,),
        in_specs=in_specs,
        out_specs=[row_spec, st_out],
        out_shape=[jax.ShapeDtypeStruct((n, D_RWKV), F32), jax.ShapeDtypeStruct(state_all.shape, F32)],
        scratch_shapes=[pltpu.VMEM((bt, D_RWKV), F32)],
        input_output_aliases=aliases,
        compiler_params=_params(("arbitrary",)),
    )(*args)


def _mix_out_kernel(yg_ref, gpool_ref, sgr_ref, x_ref, mod_ref, wbr_ref, wout_ref, lnw_ref, lnb_ref, o_ref, *, alpha):
    y_rwkv = _dot(yg_ref[...], wbr_ref[...])
    merged = gpool_ref[...].astype(F32) + sgr_ref[...].astype(F32) * y_rwkv
    mix = _dot(merged, wout_ref[...])
    g1 = mod_ref[:, 2 * D_MODEL:3 * D_MODEL]
    o_ref[...] = _ln(alpha * x_ref[...] + (1.0 + g1) * mix) * lnw_ref[...] + lnb_ref[...]


def _mix_out(yg, gpool, sgr, x2d, mod, mod_spec, grid, row_map, tm, wts, l, alpha):
    row_spec = pl.BlockSpec((tm, D_MODEL), row_map)
    return pl.pallas_call(
        functools.partial(_mix_out_kernel, alpha=alpha),
        grid=grid,
        in_specs=[row_spec] * 4 + [mod_spec, _layer_spec((D_RWKV, D_MODEL), l), _layer_spec((D_MODEL, D_MODEL), l),
                                  _layer_spec((1, D_MODEL), l), _layer_spec((1, D_MODEL), l)],
        out_specs=row_spec,
        out_shape=jax.ShapeDtypeStruct(x2d.shape, F32),
        compiler_params=_params(("arbitrary",) * len(grid)),
    )(yg, gpool, sgr, x2d, mod, *wts)


def _ffn_prompt_kernel(x_ref, mod_ref, up_ref, cw_ref, cb_ref, down_ref, lnw_ref, lnb_ref,
                       o_ref, conv_ref, carry_ref, *, tm, alpha):
    j = pl.program_id(1)
    first = j == 0
    x = x_ref[...]
    h2 = _ln(x) * (1.0 + mod_ref[:, 4 * D_MODEL:5 * D_MODEL]) + mod_ref[:, 3 * D_MODEL:4 * D_MODEL]
    hb = h2.astype(BF16)
    row = lax.broadcasted_iota(jnp.int32, (tm, 1), 0)

    @pl.when(first)
    def _():
        carry_ref[...] = jnp.zeros(carry_ref.shape, F32)

    def conv(c0, c1):
        u = jnp.dot(hb, up_ref[:, c0:c1], preferred_element_type=F32)
        c6 = carry_ref[6:7, c0:c1]
        c7 = carry_ref[7:8, c0:c1]
        p1 = jnp.where(row == 0, c7, pltpu.roll(u, 1, 0))
        p2 = jnp.where(row == 0, c6, jnp.where(row == 1, c7, pltpu.roll(u, 2, 0)))
        carry_ref[6:8, c0:c1] = u[tm - 2:tm, :]
        conv_ref[:, c0:c1] = u[tm - 2:tm, :]
        return cb_ref[:, c0:c1] + p2 * cw_ref[0:1, c0:c1] + p1 * cw_ref[1:2, c0:c1] + u * cw_ref[2:3, c0:c1]

    ff = jnp.zeros((tm, D_MODEL), F32)
    for c in range(D_FF // FF_CHUNK):
        val = conv(c * FF_CHUNK, (c + 1) * FF_CHUNK)
        gate = conv(D_FF + c * FF_CHUNK, D_FF + (c + 1) * FF_CHUNK)
        ff = ff + _dot(val * jax.nn.silu(gate), down_ref[c * FF_CHUNK:(c + 1) * FF_CHUNK, :])
    g2 = mod_ref[:, 5 * D_MODEL:6 * D_MODEL]
    o_ref[...] = _ln(alpha * x + (1.0 + g2) * ff) * lnw_ref[...] + lnb_ref[...]


def _ffn_weight_specs(l):
    return [_layer_spec((D_MODEL, 2 * D_FF), l), _layer_spec((CONV_W, 2 * D_FF), l), _layer_spec((1, 2 * D_FF), l),
            _layer_spec((D_FF, D_MODEL), l), _layer_spec((1, D_MODEL), l), _layer_spec((1, D_MODEL), l)]


def _ffn_prompt(x2d, mod3, wts, l, batch, seq, alpha):
    tm = min(ROW_TILE, seq)
    nj = seq // tm
    row_spec = pl.BlockSpec((tm, D_MODEL), lambda b, j: (b * nj + j, 0))
    return pl.pallas_call(
        functools.partial(_ffn_prompt_kernel, tm=tm, alpha=alpha),
        grid=(batch, nj),
        in_specs=[row_spec, pl.BlockSpec((None, 1, 6 * D_MODEL), lambda b, j: (b, 0, 0))] + _ffn_weight_specs(l),
        out_specs=[row_spec, pl.BlockSpec((None, CONV_W - 1, 2 * D_FF), lambda b, j: (b, 0, 0))],
        out_shape=[jax.ShapeDtypeStruct(x2d.shape, F32),
                   jax.ShapeDtypeStruct((batch, CONV_W - 1, 2 * D_FF), F32)],
        scratch_shapes=[pltpu.VMEM((8, 2 * D_FF), F32)],
        compiler_params=_params(("arbitrary", "arbitrary")),
    )(x2d, mod3, *wts)


def _ffn_sample_kernel(x_ref, mod_ref, cst_ref, up_ref, cw_ref, cb_ref, down_ref, lnw_ref, lnb_ref,
                       o_ref, conv_ref, *, alpha):
    x = x_ref[...]
    h2 = _ln(x) * (1.0 + mod_ref[:, 4 * D_MODEL:5 * D_MODEL]) + mod_ref[:, 3 * D_MODEL:4 * D_MODEL]
    hb = h2.astype(BF16)

    def conv(c0, c1):
        u = jnp.dot(hb, up_ref[:, c0:c1], preferred_element_type=F32)
        p2 = cst_ref[:, c0:c1]
        p1 = cst_ref[:, 2 * D_FF + c0:2 * D_FF + c1]
        conv_ref[:, c0:c1] = p1
        conv_ref[:, 2 * D_FF + c0:2 * D_FF + c1] = u
        return cb_ref[:, c0:c1] + p2 * cw_ref[0:1, c0:c1] + p1 * cw_ref[1:2, c0:c1] + u * cw_ref[2:3, c0:c1]

    ff = jnp.zeros(x.shape, F32)
    for c in range(D_FF // FF_CHUNK):
        val = conv(c * FF_CHUNK, (c + 1) * FF_CHUNK)
        gate = conv(D_FF + c * FF_CHUNK, D_FF + (c + 1) * FF_CHUNK)
        ff = ff + _dot(val * jax.nn.silu(gate), down_ref[c * FF_CHUNK:(c + 1) * FF_CHUNK, :])
    g2 = mod_ref[:, 5 * D_MODEL:6 * D_MODEL]
    o_ref[...] = _ln(alpha * x + (1.0 + g2) * ff) * lnw_ref[...] + lnb_ref[...]


def _ffn_sample(x2d, mod2, conv_all, wts, l, alpha):
    n = x2d.shape[0]
    cols = (CONV_W - 1) * 2 * D_FF
    out_full = lambda shape: pl.BlockSpec(shape, lambda i: (0,) * len(shape))
    return pl.pallas_call(
        functools.partial(_ffn_sample_kernel, alpha=alpha),
        grid=(1,),
        in_specs=[_const_spec((n, D_MODEL)), _const_spec((n, 6 * D_MODEL)), _layer_spec((n, cols), l)]
        + _ffn_weight_specs(l),
        out_specs=[out_full((n, D_MODEL)), out_full((n, cols))],
        out_shape=[jax.ShapeDtypeStruct(x2d.shape, F32), jax.ShapeDtypeStruct((n, cols), F32)],
        compiler_params=_params(("arbitrary",)),
    )(x2d, mod2, conv_all, *wts)


def _stacked_weights(p):
    depth = p['w_in'].shape[0]
    row = lambda a: a.reshape(depth, 1, -1)
    bf = lambda a: a.astype(BF16)
    zeros = lambda r, c: jnp.zeros((depth, r, c), F32)
    lora_up = jnp.concatenate([
        jnp.concatenate([p['decay_up'], zeros(D_DECAY_LORA, 2 * D_RWKV)], axis=2),
        jnp.concatenate([zeros(D_AAA_LORA, D_RWKV), p['aaa_up'], zeros(D_AAA_LORA, D_RWKV)], axis=2),
        jnp.concatenate([zeros(D_GATE_LORA, 2 * D_RWKV), p['gate_up']], axis=2)], axis=1)
    head = jnp.arange(D_RWKV) // HEAD_SIZE
    mhead = (head[:, None] == head[None, :]).astype(BF16)
    mix_in = (bf(p['w_in']), row(p['mu_shift']), bf(lora_up), row(p['decay_w0']), row(p['aaa_a0']),
              row(p['k_k']), row(p['k_a']), mhead, bf(p['pool_w']), row(p['pool_scale']), bf(p['w_branch_pool']))
    wkv = (row(p['r_k']), row(p['gn_w']), row(p['gn_b']))
    mix_out = (bf(p['w_branch_rwkv']), bf(p['w_out']), row(p['ln1_w']), row(p['ln1_b']))
    ffn = (bf(p['ffn_up']), p['conv_w'], row(p['conv_b']), bf(p['ffn_down']), row(p['ln2_w']), row(p['ln2_b']))
    return mix_in, wkv, mix_out, ffn


def _trunk(x_prompt, x_sample, state_shift, state_wkv, state_pool, state_conv, mod, p):
    depth = p['w_in'].shape[0]
    alpha = float((2 * depth) ** 0.25)
    batch, seq, _ = x_prompt.shape
    nsamp = x_sample.shape[0]
    xp = x_prompt.reshape(batch * seq, D_MODEL).astype(F32)
    xs = x_sample.reshape(nsamp, D_MODEL).astype(F32)
    tm = min(ROW_TILE, seq)
    nj = seq // tm
    prompt_mod_spec = pl.BlockSpec((None, 1, 6 * D_MODEL), lambda i: (i // nj, 0, 0))
    sample_mod_spec = pl.BlockSpec((nsamp, 6 * D_MODEL), lambda i: (0, 0))
    w_mix_in, w_wkv, w_mix_out, w_ffn = _stacked_weights(p)
    shift_all = state_shift.astype(F32)
    wkv_all = state_wkv.astype(F32)
    pool_all = state_pool.astype(F32).reshape(depth, nsamp, POOL_BUF * D_POOL)
    conv_all = state_conv.astype(F32).reshape(depth, nsamp, (CONV_W - 1) * 2 * D_FF)
    outs = {k: [] for k in ('shift_p', 'wkv_p', 'pool_p', 'conv_p', 'shift_s', 'pool_s', 'conv_s')}
    wkv_s = None
    for l in range(depth):
        mod_p = mod[l, 0:batch].reshape(batch, 1, 6 * D_MODEL)
        mod_s = mod[l, batch:batch + nsamp]

        *ops, gpool, sgr, shift_p, pool_p = _mix_in_prompt(xp, mod_p, w_mix_in, l, batch, seq)
        yg, wkv_p = _wkv_prompt(ops, *w_wkv, l, batch, seq)
        xp = _mix_out(yg, gpool, sgr, xp, mod_p, prompt_mod_spec, (batch * nj,), lambda i: (i, 0), tm,
                      w_mix_out, l, alpha)
        xp, conv_p = _ffn_prompt(xp, mod_p, w_ffn, l, batch, seq, alpha)
        outs['shift_p'].append(shift_p.reshape(batch, D_MODEL))
        outs['wkv_p'].append(wkv_p)
        outs['pool_p'].append(pool_p[:, 1:, :])
        outs['conv_p'].append(conv_p)

        *ops, gpool, sgr, shift_s, pool_s = _mix_in_sample(xs, mod_s, shift_all, pool_all, w_mix_in, l)
        yg, wkv_s = _wkv_sample(ops, *w_wkv, w_mix_in[7], wkv_all, wkv_s, l)
        xs = _mix_out(yg, gpool, sgr, xs, mod_s, sample_mod_spec, (1,), lambda i: (0, 0), nsamp, w_mix_out, l,
                      alpha)
        xs, conv_s = _ffn_sample(xs, mod_s, conv_all, w_ffn, l, alpha)
        outs['shift_s'].append(shift_s)
        outs['pool_s'].append(pool_s.reshape(nsamp, POOL_BUF, D_POOL))
        outs['conv_s'].append(conv_s.reshape(nsamp, CONV_W - 1, 2 * D_FF))
    st = {k: jnp.stack(v) for k, v in outs.items()}
    return (xp.reshape(batch, seq, D_MODEL), xs.reshape(nsamp, 1, D_MODEL), st['shift_p'], st['wkv_p'], st['pool_p'],
            st['conv_p'], st['shift_s'], wkv_s, st['pool_s'], st['conv_s'])


def kernel(x_prompt, x_sample, state_shift, state_wkv, state_pool, state_conv, c_prompt, c_sample, ada_w, ada_b, w_in, mu_shift, decay_w0, decay_up, aaa_a0, aaa_up, gate_up, k_k, k_a, r_k, gn_w, gn_b, w_branch_rwkv, pool_w, pool_scale, w_branch_pool, w_out, ln1_w, ln1_b, ffn_up, conv_w, conv_b, ffn_down, ln2_w, ln2_b):
    p = {'w_in': w_in, 'mu_shift': mu_shift, 'decay_w0': decay_w0, 'decay_up': decay_up, 'aaa_a0': aaa_a0,
         'aaa_up': aaa_up, 'gate_up': gate_up, 'k_k': k_k, 'k_a': k_a, 'r_k': r_k, 'gn_w': gn_w, 'gn_b': gn_b,
         'w_branch_rwkv': w_branch_rwkv, 'pool_w': pool_w, 'pool_scale': pool_scale,
         'w_branch_pool': w_branch_pool, 'w_out': w_out, 'ln1_w': ln1_w, 'ln1_b': ln1_b, 'ffn_up': ffn_up,
         'conv_w': conv_w, 'conv_b': conv_b, 'ffn_down': ffn_down, 'ln2_w': ln2_w, 'ln2_b': ln2_b}
    c_all = jnp.concatenate([c_prompt, c_sample], axis=0).astype(F32)
    mod = _modulation(c_all, ada_w, ada_b)
    dt = x_prompt.dtype
    out = _trunk(x_prompt, x_sample, state_shift, state_wkv, state_pool, state_conv, mod, p)
    return tuple(o.astype(dt) for o in out)
```

```python
import functools
import math

import jax
import jax.numpy as jnp
from jax import lax
from jax.experimental import pallas as pl
from jax.experimental.pallas import tpu as pltpu

F32 = jnp.float32
BF16 = jnp.bfloat16

D_MODEL = 1024
HEAD_SIZE = 64
N_HEADS = D_MODEL // HEAD_SIZE
D_RWKV = N_HEADS * HEAD_SIZE
D_DECAY_LORA = 64
D_AAA_LORA = 64
D_GATE_LORA = 128
D_LORA = D_DECAY_LORA + D_AAA_LORA + D_GATE_LORA
POOL_WINDOWS = (2, 4, 8, 16)
POOL_GROUP = D_MODEL // 8
D_POOL = len(POOL_WINDOWS) * POOL_GROUP
POOL_BUF = max(POOL_WINDOWS) - 1
D_FF = (D_MODEL * 11) // 4
CONV_W = 3
D_SHIFT = 3 * D_RWKV + D_LORA
IN_COLS = D_SHIFT + D_POOL + 2 * D_MODEL
LN_EPS = 1e-5
GN_EPS = 64e-5
EXP_M05 = math.exp(-0.5)

VMEM_LIMIT_BYTES = 56 * 1024 * 1024
ROW_TILE = 256
CHUNK = 64
WKV_CHUNKS_PER_STEP = 4
SAMPLE_TILE = 8
SAMPLE_SEQ_PER_ITER = 2
FF_CHUNK = 1408


def _ln(x):
    mu = jnp.mean(x, axis=-1, keepdims=True)
    xc = x - mu
    var = jnp.mean(xc * xc, axis=-1, keepdims=True)
    return xc * lax.rsqrt(var + LN_EPS)


def _dot(a, b):
    return jnp.dot(a.astype(BF16), b.astype(BF16), preferred_element_type=F32)


def _dot_nt(a, b):
    return lax.dot_general(a.astype(BF16), b.astype(BF16), (((1,), (1,)), ((), ())),
                           preferred_element_type=F32)


def _dot_tn(a, b):
    return lax.dot_general(a.astype(BF16), b.astype(BF16), (((0,), (0,)), ((), ())),
                           preferred_element_type=F32)


def _const_spec(shape):
    nd = len(shape)
    return pl.BlockSpec(shape, lambda *_: (0,) * nd, pipeline_mode=pl.Buffered(1))


def _layer_spec(shape, l):
    nd = len(shape)
    return pl.BlockSpec((None,) + tuple(shape), lambda *_: (l,) + (0,) * nd, pipeline_mode=pl.Buffered(1))


def _params(sem):
    return pltpu.CompilerParams(dimension_semantics=sem, vmem_limit_bytes=VMEM_LIMIT_BYTES)


def _mod_kernel(c_ref, w_ref, b_ref, o_ref):
    o_ref[...] = _dot(c_ref[...], w_ref[...]) + b_ref[...]


def _modulation(c_all, ada_w, ada_b):
    depth, d, n = ada_w.shape
    rows = c_all.shape[0]
    tn = 1536
    return pl.pallas_call(
        _mod_kernel,
        grid=(depth, n // tn),
        in_specs=[pl.BlockSpec((rows, d), lambda l, j: (0, 0)),
                  pl.BlockSpec((None, d, tn), lambda l, j: (l, 0, j)),
                  pl.BlockSpec((None, 1, tn), lambda l, j: (l, 0, j))],
        out_specs=pl.BlockSpec((None, rows, tn), lambda l, j: (l, 0, j)),
        out_shape=jax.ShapeDtypeStruct((depth, rows, n), F32),
        compiler_params=_params(("arbitrary", "arbitrary")),
    )(c_all, ada_w, ada_b.reshape(depth, 1, n))


def _head_sum(x, mpair):
    rows = x.shape[0]
    tiles = range(x.shape[1] // (2 * HEAD_SIZE))
    st = jnp.concatenate([x[:, t * 2 * HEAD_SIZE:(t + 1) * 2 * HEAD_SIZE] for t in tiles], axis=0)
    sm = _dot(st, mpair)
    return jnp.concatenate([sm[t * rows:(t + 1) * rows] for t in tiles], axis=1)


def _rwkv_prep_dots(xs_k, lora, wl_ref, kk_ref, mpair_ref):
    kk = xs_k * kk_ref[...]
    ss = _head_sum(kk * kk, mpair_ref[...])
    lane = lax.broadcasted_iota(jnp.int32, lora.shape, 1)
    act = jnp.where(lane < D_DECAY_LORA, jnp.tanh(lora),
                    jnp.where(lane < D_DECAY_LORA + D_AAA_LORA, lora, jax.nn.sigmoid(lora)))
    z = _dot(act, wl_ref[...])
    return z, kk, ss


def _rwkv_prep_finish(xs_r, xs_k, xs_v, z, kk, ss, w0_ref, a0_ref, ka_ref):
    dec = w0_ref[...] + z[:, 0:D_RWKV]
    lw = -jax.nn.sigmoid(dec) * EXP_M05
    a = jax.nn.sigmoid(a0_ref[...] + z[:, D_RWKV:2 * D_RWKV])
    g = z[:, 2 * D_RWKV:3 * D_RWKV]
    kap = kk / jnp.maximum(jnp.sqrt(ss), 1e-12)
    kmod = xs_k * (1.0 + (a - 1.0) * ka_ref[...])
    return xs_r, kmod, xs_v, kap, kap * a, lw, g


def _pool_mix(pooled, p_in, poolw_ref, pscale_ref, wbp_ref):
    mixed = []
    for gi in range(len(POOL_WINDOWS)):
        sl = slice(gi * POOL_GROUP, (gi + 1) * POOL_GROUP)
        diff = pooled[gi] - p_in[:, sl]
        mixed.append(_dot(diff, poolw_ref[gi]) * pscale_ref[:, sl])
    return _dot(jnp.concatenate(mixed, axis=-1), wbp_ref[...])


def _mix_in_prompt_kernel(x_ref, mod_ref, win_ref, mu_ref, wl_ref, w0_ref, a0_ref, kk_ref, ka_ref,
                          mpair_ref, poolw_ref, pscale_ref, wbp_ref,
                          r_ref, k_ref, v_ref, kap_ref, b_ref, lw_ref, g_ref, gpool_ref, sgr_ref,
                          shift_ref, pool_ref,
                          carry_ref, pext_ref, *, tm):
    j = pl.program_id(1)

    @pl.when(j == 0)
    def _():
        carry_ref[...] = jnp.zeros(carry_ref.shape, F32)
        pext_ref[...] = jnp.zeros(pext_ref.shape, F32)

    x = x_ref[...]
    h = _ln(x) * (1.0 + mod_ref[:, D_MODEL:2 * D_MODEL]) + mod_ref[:, 0:D_MODEL]
    shift_ref[...] = h[tm - 1:tm, :]
    hb = h.astype(BF16)
    row = lax.broadcasted_iota(jnp.int32, (tm, 1), 0)

    def shifted(c0, c1):
        cur = jnp.dot(hb, win_ref[:, c0:c1], preferred_element_type=F32)
        prev = jnp.where(row == 0, carry_ref[7:8, c0:c1], pltpu.roll(cur, 1, 0))
        carry_ref[7:8, c0:c1] = cur[tm - 1:tm, :]
        return cur + (prev - cur) * mu_ref[:, c0:c1]

    xs_k = shifted(D_RWKV, 2 * D_RWKV)
    lora = shifted(3 * D_RWKV, D_SHIFT)
    z, kk, ss = _rwkv_prep_dots(xs_k, lora, wl_ref, kk_ref, mpair_ref)
    p_in = jnp.dot(hb, win_ref[:, D_SHIFT:D_SHIFT + D_POOL], preferred_element_type=F32)
    gates = jnp.dot(hb, win_ref[:, D_SHIFT + D_POOL:IN_COLS], preferred_element_type=F32)

    pext_ref[0:16, :] = pext_ref[tm:tm + 16, :]
    pext_ref[16:16 + tm, :] = p_in
    pos1 = (j * tm + row + 1).astype(F32)
    pooled = []
    for gi, win in enumerate(POOL_WINDOWS):
        sl = slice(gi * POOL_GROUP, (gi + 1) * POOL_GROUP)
        acc = p_in[:, sl]
        for d in range(1, win):
            acc = acc + pext_ref[16 - d:16 - d + tm, sl]
        pooled.append(acc / jnp.minimum(float(win), pos1))
    pool_ref[...] = pext_ref[tm:tm + 16, :]
    y_pool = _pool_mix(pooled, p_in, poolw_ref, pscale_ref, wbp_ref)

    xs_r = shifted(0, D_RWKV)
    xs_v = shifted(2 * D_RWKV, 3 * D_RWKV)
    r, kmod, v, kap, b, lw, g = _rwkv_prep_finish(xs_r, xs_k, xs_v, z, kk, ss, w0_ref, a0_ref, ka_ref)
    r_ref[...] = r
    k_ref[...] = kmod
    v_ref[...] = v
    kap_ref[...] = kap
    b_ref[...] = b
    lw_ref[...] = lw
    g_ref[...] = g.astype(BF16)
    sgr_ref[...] = jax.nn.sigmoid(gates[:, D_MODEL:2 * D_MODEL]).astype(BF16)
    gpool_ref[...] = (jax.nn.sigmoid(gates[:, 0:D_MODEL]) * y_pool).astype(BF16)


def _mix_in_sample_kernel(x_ref, mod_ref, sprev_ref, pst_ref, win_ref, mu_ref, wl_ref, w0_ref, a0_ref, kk_ref,
                          ka_ref, mpair_ref, poolw_ref, pscale_ref, wbp_ref,
                          r_ref, k_ref, v_ref, kap_ref, b_ref, w_ref, g_ref, gpool_ref, sgr_ref,
                          shift_ref, pool_ref):
    x = x_ref[...]
    h = _ln(x) * (1.0 + mod_ref[:, D_MODEL:2 * D_MODEL]) + mod_ref[:, 0:D_MODEL]
    shift_ref[...] = h
    hb = h.astype(BF16)
    pb = sprev_ref[...].astype(BF16)

    def shifted(c0, c1):
        w = win_ref[:, c0:c1]
        cur = jnp.dot(hb, w, preferred_element_type=F32)
        prev = jnp.dot(pb, w, preferred_element_type=F32)
        return cur + (prev - cur) * mu_ref[:, c0:c1]

    xs_r = shifted(0, D_RWKV)
    xs_k = shifted(D_RWKV, 2 * D_RWKV)
    xs_v = shifted(2 * D_RWKV, 3 * D_RWKV)
    lora = shifted(3 * D_RWKV, D_SHIFT)
    z, kk, ss = _rwkv_prep_dots(xs_k, lora, wl_ref, kk_ref, mpair_ref)
    r, kmod, v, kap, b, lw, g = _rwkv_prep_finish(xs_r, xs_k, xs_v, z, kk, ss, w0_ref, a0_ref, ka_ref)
    r_ref[...] = r
    k_ref[...] = kmod
    v_ref[...] = v
    kap_ref[...] = kap
    b_ref[...] = b
    w_ref[...] = jnp.exp(lw)
    g_ref[...] = g

    p_in = jnp.dot(hb, win_ref[:, D_SHIFT:D_SHIFT + D_POOL], preferred_element_type=F32)
    pooled = []
    for gi, win in enumerate(POOL_WINDOWS):
        sl = slice(gi * POOL_GROUP, (gi + 1) * POOL_GROUP)
        acc = p_in[:, sl]
        for d in range(1, win):
            acc = acc + pst_ref[:, POOL_BUF - d, sl]
        pooled.append(acc / float(win))
    gates = jnp.dot(hb, win_ref[:, D_SHIFT + D_POOL:IN_COLS], preferred_element_type=F32)
    gpool = jax.nn.sigmoid(gates[:, 0:D_MODEL]) * _pool_mix(pooled, p_in, poolw_ref, pscale_ref, wbp_ref)
    gpool_ref[...] = gpool.astype(BF16)
    sgr_ref[...] = jax.nn.sigmoid(gates[:, D_MODEL:2 * D_MODEL]).astype(BF16)
    for i in range(POOL_BUF - 1):
        pool_ref[:, i, :] = pst_ref[:, i + 1, :]
    pool_ref[:, POOL_BUF - 1, :] = p_in


def _mix_in_weight_specs(l):
    return [_layer_spec((D_MODEL, IN_COLS), l),
            _layer_spec((1, D_SHIFT), l),
            _layer_spec((D_LORA, 3 * D_RWKV), l),
            _layer_spec((1, D_RWKV), l),
            _layer_spec((1, D_RWKV), l),
            _layer_spec((1, D_RWKV), l),
            _layer_spec((1, D_RWKV), l),
            _const_spec((2 * HEAD_SIZE, 2 * HEAD_SIZE)),
            _layer_spec((len(POOL_WINDOWS), POOL_GROUP, POOL_GROUP), l),
            _layer_spec((1, D_POOL), l),
            _layer_spec((D_POOL, D_MODEL), l)]


def _mix_in_prompt(x2d, mod3, wts, l, batch, seq):
    tm = min(ROW_TILE, seq)
    nj = seq // tm
    rows = batch * seq
    row_spec = pl.BlockSpec((tm, D_MODEL), lambda b, j: (b * nj + j, 0))
    f32_rows = jax.ShapeDtypeStruct((rows, D_MODEL), F32)
    bf_rows = jax.ShapeDtypeStruct((rows, D_MODEL), BF16)
    return pl.pallas_call(
        functools.partial(_mix_in_prompt_kernel, tm=tm),
        grid=(batch, nj),
        in_specs=[row_spec, pl.BlockSpec((None, 1, 6 * D_MODEL), lambda b, j: (b, 0, 0))] + _mix_in_weight_specs(l),
        out_specs=[row_spec] * 9 + [pl.BlockSpec((None, 1, D_MODEL), lambda b, j: (b, 0, 0)),
                                    pl.BlockSpec((None, 16, D_POOL), lambda b, j: (b, 0, 0))],
        out_shape=[f32_rows] * 6 + [bf_rows] * 3 + [jax.ShapeDtypeStruct((batch, 1, D_MODEL), F32),
                                                    jax.ShapeDtypeStruct((batch, 16, D_POOL), F32)],
        scratch_shapes=[pltpu.VMEM((8, D_SHIFT), F32), pltpu.VMEM((tm + 16, D_POOL), F32)],
        compiler_params=_params(("arbitrary", "arbitrary")),
    )(x2d, mod3, *wts)


def _mix_in_sample(x2d, mod2, shift_all, pool_all, wts, l):
    n = x2d.shape[0]
    out_full = lambda shape: pl.BlockSpec(shape, lambda i: (0,) * len(shape))
    f32_rows = jax.ShapeDtypeStruct((n, D_MODEL), F32)
    bf_rows = jax.ShapeDtypeStruct((n, D_MODEL), BF16)
    return pl.pallas_call(
        _mix_in_sample_kernel,
        grid=(1,),
        in_specs=[_const_spec((n, D_MODEL)), _const_spec((n, 6 * D_MODEL)), _layer_spec((n, D_MODEL), l),
                  _layer_spec((n, POOL_BUF, D_POOL), l)] + _mix_in_weight_specs(l),
        out_specs=[out_full((n, D_MODEL))] * 10 + [out_full((n, POOL_BUF, D_POOL))],
        out_shape=[f32_rows] * 7 + [bf_rows] * 2 + [f32_rows, jax.ShapeDtypeStruct((n, POOL_BUF, D_POOL), F32)],
        compiler_params=_params(("arbitrary",)),
    )(x2d, mod2, shift_all, pool_all, *wts)


def _wkv_prompt_kernel(r_ref, k_ref, v_ref, kap_ref, b_ref, lw_ref, g_ref, rk_ref, gnw_ref, gnb_ref, mpair_ref,
                       yg_ref, snew_ref, s_ref, *, nch):
    @pl.when(pl.program_id(1) == 0)
    def _():
        s_ref[...] = jnp.zeros(s_ref.shape, F32)

    L = CHUNK
    pw2 = 2 * HEAD_SIZE
    lane = lax.broadcasted_iota(jnp.int32, (L, pw2), 1)
    trow = lax.broadcasted_iota(jnp.int32, (L, pw2), 0)
    left = lane < HEAD_SIZE
    src = jnp.where(left, lane, lane - HEAD_SIZE)
    strict = src < trow
    incl = src <= trow
    eye = (src == trow).astype(F32)
    bd_mask = ((lax.broadcasted_iota(jnp.int32, (pw2, pw2), 0) < HEAD_SIZE)
               == (lax.broadcasted_iota(jnp.int32, (pw2, pw2), 1) < HEAD_SIZE))

    crow = lax.broadcasted_iota(jnp.int32, (L, 1), 0)

    def bd(x):
        xb = x.astype(BF16)
        zero = jnp.zeros_like(xb)
        return jnp.concatenate([jnp.where(left, xb, zero), jnp.where(left, zero, xb)], axis=0)

    def cumsum_rows(x):
        sh = 1
        while sh < L:
            x = x + jnp.where(crow >= sh, pltpu.roll(x, sh, 0), 0.0)
            sh *= 2
        return x

    chunks = range(nch)
    pairs = range(N_HEADS // 2)
    cs = [slice(p * pw2, (p + 1) * pw2) for p in pairs]
    units = [(j, p) for j in chunks for p in pairs]

    r, kmod, v, g, dec_last, a_hat, r_hat, b_hat, k_hat, b_til, k_til = ({} for _ in range(11))
    for j in chunks:
        rs = slice(j * L, (j + 1) * L)
        lw = lw_ref[rs, :]
        p_inc = cumsum_rows(lw)
        p_last = p_inc[L - 1:L, :]
        e_p = jnp.exp(p_inc)
        e_np = jnp.exp(-p_inc)
        e_l = jnp.exp(p_last - p_inc)
        dec_last[j] = jnp.exp(p_last)
        r[j], kmod[j], v[j], bb = r_ref[rs, :], k_ref[rs, :], v_ref[rs, :], b_ref[rs, :]
        g[j] = g_ref[rs, :].astype(F32)
        r_hat[j] = r[j] * e_p
        a_hat[j] = -kap_ref[rs, :] * jnp.exp(p_inc - lw)
        b_hat[j] = bb * e_np
        k_hat[j] = kmod[j] * e_np
        b_til[j] = bb * e_l
        k_til[j] = kmod[j] * e_l

    ar = {(j, p): jnp.concatenate([a_hat[j][:, cs[p]], r_hat[j][:, cs[p]]], axis=0).astype(BF16) for j, p in units}
    gram = {(j, p): _dot_nt(ar[j, p], jnp.concatenate([bd(b_hat[j][:, cs[p]]), bd(k_hat[j][:, cs[p]])], axis=0))
            for j, p in units}
    nab = {u_: jnp.where(strict, gram[u_][0:L, 0:pw2], 0.0) for u_ in units}
    nak = {u_: jnp.where(strict, gram[u_][0:L, pw2:2 * pw2], 0.0) for u_ in units}
    rbk = {u_: jnp.concatenate([jnp.where(incl, gram[u_][L:2 * L, 0:pw2], 0.0),
                                jnp.where(incl, gram[u_][L:2 * L, pw2:2 * pw2], 0.0)], axis=1).astype(BF16)
           for u_ in units}
    inv = {u_: eye + nab[u_] for u_ in units}
    pw = {u_: _dot(nab[u_], bd(nab[u_])) for u_ in units}
    for _ in range(4):
        px = {u_: _dot(jnp.concatenate([pw[u_], inv[u_]], axis=0), bd(pw[u_])) for u_ in units}
        pw = {u_: px[u_][0:L] for u_ in units}
        inv = {u_: inv[u_] + px[u_][L:2 * L] for u_ in units}
    inv = {u_: inv[u_] + _dot(inv[u_], bd(pw[u_])) for u_ in units}
    vp = {(j, p): v[j][:, cs[p]] for j, p in units}
    bdv = {u_: bd(vp[u_]) for u_ in units}
    akv = {u_: _dot(nak[u_], bdv[u_]) for u_ in units}

    s_cur = [s_ref[p] for p in pairs]
    y = {}
    for j in chunks:
        ars = [_dot_nt(ar[j, p], s_cur[p]) for p in pairs]
        u = [_dot(inv[j, p], bd(ars[p][0:L] + akv[j, p])) for p in pairs]
        for p in pairs:
            y[j, p] = ars[p][L:2 * L] + _dot(rbk[j, p], jnp.concatenate([bd(u[p]), bdv[j, p]], axis=0))
        upd = [_dot_tn(jnp.concatenate([u[p], vp[j, p]], axis=0),
                       jnp.concatenate([b_til[j][:, cs[p]], k_til[j][:, cs[p]]], axis=0)) for p in pairs]
        s_cur = [s_cur[p] * dec_last[j][:, cs[p]] + jnp.where(bd_mask, upd[p], 0.0) for p in pairs]
    for p in pairs:
        s_ref[p] = s_cur[p]
        snew_ref[p] = s_cur[p]

    inv_n = 1.0 / HEAD_SIZE
    mpair = mpair_ref[...]
    for j in chunks:
        y_all = jnp.concatenate([y[j, p] for p in pairs], axis=1)
        mu = _head_sum(y_all, mpair) * inv_n
        yc = y_all - mu
        var = _head_sum(yc * yc, mpair) * inv_n
        yn = yc * lax.rsqrt(var + GN_EPS) * gnw_ref[...] + gnb_ref[...]
        bonus = _head_sum(r[j] * kmod[j] * rk_ref[...], mpair) * v[j]
        yg_ref[j * L:(j + 1) * L, :] = ((yn + bonus) * g[j]).astype(BF16)


def _wkv_prompt(ops, rk, gnw, gnb, mpair, l, batch, seq):
    nch = math.gcd(seq // CHUNK, WKV_CHUNKS_PER_STEP)
    rows_per_step = nch * CHUNK
    nc = seq // rows_per_step
    rows = batch * seq
    npair = N_HEADS // 2
    pw2 = 2 * HEAD_SIZE
    row_spec = pl.BlockSpec((rows_per_step, D_RWKV), lambda b, c: (b * nc + c, 0))
    yg, sbd = pl.pallas_call(
        functools.partial(_wkv_prompt_kernel, nch=nch),
        grid=(batch, nc),
        in_specs=[row_spec] * 7 + [_layer_spec((1, D_RWKV), l)] * 3 + [_const_spec((pw2, pw2))],
        out_specs=[row_spec, pl.BlockSpec((None, npair, pw2, pw2), lambda b, c: (b, 0, 0, 0))],
        out_shape=[jax.ShapeDtypeStruct((rows, D_RWKV), BF16),
                   jax.ShapeDtypeStruct((batch, npair, pw2, pw2), F32)],
        scratch_shapes=[pltpu.VMEM((npair, pw2, pw2), F32)],
        compiler_params=_params(("arbitrary", "arbitrary")),
    )(*ops, rk, gnw, gnb, mpair)
    n = HEAD_SIZE
    wkv = jnp.stack([sbd[:, :, 0:n, 0:n], sbd[:, :, n:2 * n, n:2 * n]], axis=2)
    return yg, wkv.reshape(batch, N_HEADS, n, n)


def _wkv_sample_kernel(r_ref, k_ref, v_ref, kap_ref, b_ref, w_ref, g_ref, rk_ref, gnw_ref, gnb_ref, mpair_ref, s_ref,
                       *rest):
    *prev_refs, yg_ref, snew_ref, y_ref = rest
    slot = len(prev_refs)
    n = HEAD_SIZE
    heads = range(N_HEADS)
    sls = [slice(hd * n, (hd + 1) * n) for hd in heads]
    for j, prev_ref in enumerate(prev_refs):
        snew_ref[j] = prev_ref[...]

    def body(it, carry):
        pad = jnp.zeros((6, n), F32)
        seqs = [it * SAMPLE_SEQ_PER_ITER + q for q in range(SAMPLE_SEQ_PER_ITER)]
        rows = [pl.ds(i, 1) for i in seqs]
        r_row, k_row, v_row = ([ref[rw, :] for rw in rows] for ref in (r_ref, k_ref, v_ref))
        kap_row, w_row = ([ref[rw, :] for rw in rows] for ref in (kap_ref, w_ref))
        nb_row = [-b_ref[rw, :] for rw in rows]
        units = [(q, h) for q in range(SAMPLE_SEQ_PER_ITER) for h in heads]
        s_in = {(q, h): s_ref[seqs[q], h] for q, h in units}
        sa = {(q, h): _dot_nt(kap_row[q][:, sls[h]], s_in[q, h]) for q, h in units}
        upd = {(q, h): _dot_tn(jnp.concatenate([sa[q, h], v_row[q][:, sls[h]], pad], axis=0),
                               jnp.concatenate([nb_row[q][:, sls[h]], k_row[q][:, sls[h]], pad], axis=0))
               for q, h in units}
        s_new = {(q, h): s_in[q, h] * w_row[q][:, sls[h]] + upd[q, h] for q, h in units}
        for q, h in units:
            snew_ref[slot, seqs[q], h] = s_new[q, h]
        y = {(q, h): _dot_nt(r_row[q][:, sls[h]], s_new[q, h]) for q, h in units}
        for q in range(SAMPLE_SEQ_PER_ITER):
            y_ref[rows[q], :] = jnp.concatenate([y[q, h] for h in heads], axis=-1)
        return carry

    lax.fori_loop(0, SAMPLE_TILE // SAMPLE_SEQ_PER_ITER, body, 0)

    inv_n = 1.0 / n
    mpair = mpair_ref[...]
    y = y_ref[...]
    r, kmod, v = r_ref[...], k_ref[...], v_ref[...]
    mu = _head_sum(y, mpair) * inv_n
    yc = y - mu
    var = _head_sum(yc * yc, mpair) * inv_n
    yn = yc * lax.rsqrt(var + GN_EPS) * gnw_ref[...] + gnb_ref[...]
    bonus = _head_sum(r * kmod * rk_ref[...], mpair) * v
    yg_ref[...] = (yn + bonus) * g_ref[...]


def _wkv_sample(ops, rk, gnw, gnb, mpair, state_all, prev_new, l):
    depth, n = state_all.shape[0], state_all.shape[1]
    bt = min(SAMPLE_TILE, n)
    prev = list(prev_new) if l == depth - 1 else []
    slots = len(prev) + 1
    row_spec = pl.BlockSpec((bt, D_RWKV), lambda i: (i, 0))
    tile = (bt, N_HEADS, HEAD_SIZE, HEAD_SIZE)
    st_in = pl.BlockSpec((None,) + tile, lambda i: (l, i, 0, 0, 0))
    prev_spec = pl.BlockSpec((None,) + tile, lambda i: (0, i, 0, 0, 0))
    st_out = pl.BlockSpec((slots,) + tile, lambda i: (0, i, 0, 0, 0))
    return pl.pallas_call(
        _wkv_sample_kernel,
        grid=(n // bt,),
        in_specs=[row_spec] * 7 + [_layer_spec((1, D_RWKV), l)] * 3
        + [_const_spec((2 * HEAD_SIZE, 2 * HEAD_SIZE)), st_in] + [prev_spec] * len(prev),
        out_specs=[row_spec, st_out],
        out_shape=[jax.ShapeDtypeStruct((n, D_RWKV), F32),
                   jax.ShapeDtypeStruct((slots, n, N_HEADS, HEAD_SIZE, HEAD_SIZE), F32)],
        scratch_shapes=[pltpu.VMEM((bt, D_RWKV), F32)],
        compiler_params=_params(("arbitrary",)),
    )(*ops, rk, gnw, gnb, mpair, state_all, *prev)


def _mix_out_kernel(yg_ref, gpool_ref, sgr_ref, x_ref, mod_ref, wbr_ref, wout_ref, lnw_ref, lnb_ref, o_ref, *, alpha):
    y_rwkv = _dot(yg_ref[...], wbr_ref[...])
    merged = gpool_ref[...].astype(F32) + sgr_ref[...].astype(F32) * y_rwkv
    mix = _dot(merged, wout_ref[...])
    g1 = mod_ref[:, 2 * D_MODEL:3 * D_MODEL]
    o_ref[...] = _ln(alpha * x_ref[...] + (1.0 + g1) * mix) * lnw_ref[...] + lnb_ref[...]


def _mix_out(yg, gpool, sgr, x2d, mod, mod_spec, grid, row_map, tm, wts, l, alpha):
    row_spec = pl.BlockSpec((tm, D_MODEL), row_map)
    return pl.pallas_call(
        functools.partial(_mix_out_kernel, alpha=alpha),
        grid=grid,
        in_specs=[row_spec] * 4 + [mod_spec, _layer_spec((D_RWKV, D_MODEL), l), _layer_spec((D_MODEL, D_MODEL), l),
                                  _layer_spec((1, D_MODEL), l), _layer_spec((1, D_MODEL), l)],
        out_specs=row_spec,
        out_shape=jax.ShapeDtypeStruct(x2d.shape, F32),
        compiler_params=_params(("arbitrary",) * len(grid)),
    )(yg, gpool, sgr, x2d, mod, *wts)


def _ffn_prompt_kernel(x_ref, mod_ref, up_ref, cw_ref, cb_ref, down_ref, lnw_ref, lnb_ref,
                       o_ref, conv_ref, carry_ref, *, tm, alpha):
    j = pl.program_id(1)
    first = j == 0
    x = x_ref[...]
    h2 = _ln(x) * (1.0 + mod_ref[:, 4 * D_MODEL:5 * D_MODEL]) + mod_ref[:, 3 * D_MODEL:4 * D_MODEL]
    hb = h2.astype(BF16)
    row = lax.broadcasted_iota(jnp.int32, (tm, 1), 0)

    @pl.when(first)
    def _():
        carry_ref[...] = jnp.zeros(carry_ref.shape, F32)

    def conv(c0, c1):
        u = jnp.dot(hb, up_ref[:, c0:c1], preferred_element_type=F32)
        c6 = carry_ref[6:7, c0:c1]
        c7 = carry_ref[7:8, c0:c1]
        p1 = jnp.where(row == 0, c7, pltpu.roll(u, 1, 0))
        p2 = jnp.where(row == 0, c6, jnp.where(row == 1, c7, pltpu.roll(u, 2, 0)))
        carry_ref[6:8, c0:c1] = u[tm - 2:tm, :]
        conv_ref[:, c0:c1] = u[tm - 2:tm, :]
        return cb_ref[:, c0:c1] + p2 * cw_ref[0:1, c0:c1] + p1 * cw_ref[1:2, c0:c1] + u * cw_ref[2:3, c0:c1]

    nchunk = D_FF // FF_CHUNK
    val = [conv(c * FF_CHUNK, (c + 1) * FF_CHUNK) for c in range(nchunk)]
    gate = [conv(D_FF + c * FF_CHUNK, D_FF + (c + 1) * FF_CHUNK) for c in range(nchunk)]
    ff = jnp.zeros((tm, D_MODEL), F32)
    for c in range(nchunk):
        ff = ff + _dot(val[c] * jax.nn.silu(gate[c]), down_ref[c * FF_CHUNK:(c + 1) * FF_CHUNK, :])
    g2 = mod_ref[:, 5 * D_MODEL:6 * D_MODEL]
    o_ref[...] = _ln(alpha * x + (1.0 + g2) * ff) * lnw_ref[...] + lnb_ref[...]


def _ffn_weight_specs(l):
    return [_layer_spec((D_MODEL, 2 * D_FF), l), _layer_spec((CONV_W, 2 * D_FF), l), _layer_spec((1, 2 * D_FF), l),
            _layer_spec((D_FF, D_MODEL), l), _layer_spec((1, D_MODEL), l), _layer_spec((1, D_MODEL), l)]


def _ffn_prompt(x2d, mod3, wts, l, batch, seq, alpha):
    tm = min(ROW_TILE, seq)
    nj = seq // tm
    row_spec = pl.BlockSpec((tm, D_MODEL), lambda b, j: (b * nj + j, 0))
    return pl.pallas_call(
        functools.partial(_ffn_prompt_kernel, tm=tm, alpha=alpha),
        grid=(batch, nj),
        in_specs=[row_spec, pl.BlockSpec((None, 1, 6 * D_MODEL), lambda b, j: (b, 0, 0))] + _ffn_weight_specs(l),
        out_specs=[row_spec, pl.BlockSpec((None, CONV_W - 1, 2 * D_FF), lambda b, j: (b, 0, 0))],
        out_shape=[jax.ShapeDtypeStruct(x2d.shape, F32),
                   jax.ShapeDtypeStruct((batch, CONV_W - 1, 2 * D_FF), F32)],
        scratch_shapes=[pltpu.VMEM((8, 2 * D_FF), F32)],
        compiler_params=_params(("arbitrary", "arbitrary")),
    )(x2d, mod3, *wts)


def _ffn_sample_kernel(x_ref, mod_ref, cst_ref, up_ref, cw_ref, cb_ref, down_ref, lnw_ref, lnb_ref,
                       o_ref, conv_ref, *, alpha):
    x = x_ref[...]
    h2 = _ln(x) * (1.0 + mod_ref[:, 4 * D_MODEL:5 * D_MODEL]) + mod_ref[:, 3 * D_MODEL:4 * D_MODEL]
    hb = h2.astype(BF16)

    def conv(c0, c1):
        u = jnp.dot(hb, up_ref[:, c0:c1], preferred_element_type=F32)
        p2 = cst_ref[:, 0, c0:c1]
        p1 = cst_ref[:, 1, c0:c1]
        conv_ref[:, 0, c0:c1] = p1
        conv_ref[:, 1, c0:c1] = u
        return cb_ref[:, c0:c1] + p2 * cw_ref[0:1, c0:c1] + p1 * cw_ref[1:2, c0:c1] + u * cw_ref[2:3, c0:c1]

    ff = jnp.zeros(x.shape, F32)
    for c in range(D_FF // FF_CHUNK):
        val = conv(c * FF_CHUNK, (c + 1) * FF_CHUNK)
        gate = conv(D_FF + c * FF_CHUNK, D_FF + (c + 1) * FF_CHUNK)
        ff = ff + _dot(val * jax.nn.silu(gate), down_ref[c * FF_CHUNK:(c + 1) * FF_CHUNK, :])
    g2 = mod_ref[:, 5 * D_MODEL:6 * D_MODEL]
    o_ref[...] = _ln(alpha * x + (1.0 + g2) * ff) * lnw_ref[...] + lnb_ref[...]


def _ffn_sample(x2d, mod2, conv_all, wts, l, alpha):
    n = x2d.shape[0]
    st = (n, CONV_W - 1, 2 * D_FF)
    out_full = lambda shape: pl.BlockSpec(shape, lambda i: (0,) * len(shape))
    return pl.pallas_call(
        functools.partial(_ffn_sample_kernel, alpha=alpha),
        grid=(1,),
        in_specs=[_const_spec((n, D_MODEL)), _const_spec((n, 6 * D_MODEL)), _layer_spec(st, l)]
        + _ffn_weight_specs(l),
        out_specs=[out_full((n, D_MODEL)), out_full(st)],
        out_shape=[jax.ShapeDtypeStruct(x2d.shape, F32), jax.ShapeDtypeStruct(st, F32)],
        compiler_params=_params(("arbitrary",)),
    )(x2d, mod2, conv_all, *wts)


def _stacked_weights(p):
    depth = p['w_in'].shape[0]
    row = lambda a: a.reshape(depth, 1, -1)
    bf = lambda a: a.astype(BF16)
    zeros = lambda r, c: jnp.zeros((depth, r, c), F32)
    lora_up = jnp.concatenate([
        jnp.concatenate([p['decay_up'], zeros(D_DECAY_LORA, 2 * D_RWKV)], axis=2),
        jnp.concatenate([zeros(D_AAA_LORA, D_RWKV), p['aaa_up'], zeros(D_AAA_LORA, D_RWKV)], axis=2),
        jnp.concatenate([zeros(D_GATE_LORA, 2 * D_RWKV), p['gate_up']], axis=2)], axis=1)
    head = jnp.arange(2 * HEAD_SIZE) // HEAD_SIZE
    mpair = (head[:, None] == head[None, :]).astype(BF16)
    mix_in = (bf(p['w_in']), row(p['mu_shift']), bf(lora_up), row(p['decay_w0']), row(p['aaa_a0']),
              row(p['k_k']), row(p['k_a']), mpair, bf(p['pool_w']), row(p['pool_scale']), bf(p['w_branch_pool']))
    wkv = (row(p['r_k']), row(p['gn_w']), row(p['gn_b']))
    mix_out = (bf(p['w_branch_rwkv']), bf(p['w_out']), row(p['ln1_w']), row(p['ln1_b']))
    ffn = (bf(p['ffn_up']), p['conv_w'], row(p['conv_b']), bf(p['ffn_down']), row(p['ln2_w']), row(p['ln2_b']))
    return mix_in, wkv, mix_out, ffn


def _trunk(x_prompt, x_sample, state_shift, state_wkv, state_pool, state_conv, mod, p):
    depth = p['w_in'].shape[0]
    alpha = float((2 * depth) ** 0.25)
    batch, seq, _ = x_prompt.shape
    nsamp = x_sample.shape[0]
    xp = x_prompt.reshape(batch * seq, D_MODEL).astype(F32)
    xs = x_sample.reshape(nsamp, D_MODEL).astype(F32)
    tm = min(ROW_TILE, seq)
    nj = seq // tm
    prompt_mod_spec = pl.BlockSpec((None, 1, 6 * D_MODEL), lambda i: (i // nj, 0, 0))
    sample_mod_spec = pl.BlockSpec((nsamp, 6 * D_MODEL), lambda i: (0, 0))
    w_mix_in, w_wkv, w_mix_out, w_ffn = _stacked_weights(p)
    shift_all = state_shift.astype(F32)
    wkv_all = state_wkv.astype(F32)
    pool_all = state_pool.astype(F32)
    conv_all = state_conv.astype(F32)
    outs = {k: [] for k in ('shift_p', 'wkv_p', 'pool_p', 'conv_p', 'shift_s', 'wkv_s', 'pool_s', 'conv_s')}
    for l in range(depth):
        mod_p = mod[l, 0:batch].reshape(batch, 1, 6 * D_MODEL)
        mod_s = mod[l, batch:batch + nsamp]

        *ops, gpool, sgr, shift_p, pool_p = _mix_in_prompt(xp, mod_p, w_mix_in, l, batch, seq)
        yg, wkv_p = _wkv_prompt(ops, *w_wkv, w_mix_in[7], l, batch, seq)
        xp = _mix_out(yg, gpool, sgr, xp, mod_p, prompt_mod_spec, (batch * nj,), lambda i: (i, 0), tm,
                      w_mix_out, l, alpha)
        xp, conv_p = _ffn_prompt(xp, mod_p, w_ffn, l, batch, seq, alpha)
        outs['shift_p'].append(shift_p.reshape(batch, D_MODEL))
        outs['wkv_p'].append(wkv_p)
        outs['pool_p'].append(pool_p[:, 1:, :])
        outs['conv_p'].append(conv_p)

        *ops, gpool, sgr, shift_s, pool_s = _mix_in_sample(xs, mod_s, shift_all, pool_all, w_mix_in, l)
        yg, wkv_s = _wkv_sample(ops, *w_wkv, w_mix_in[7], wkv_all, outs['wkv_s'], l)
        xs = _mix_out(yg, gpool, sgr, xs, mod_s, sample_mod_spec, (1,), lambda i: (0, 0), nsamp, w_mix_out, l,
                      alpha)
        xs, conv_s = _ffn_sample(xs, mod_s, conv_all, w_ffn, l, alpha)
        outs['shift_s'].append(shift_s)
        outs['wkv_s'].append(wkv_s)
        outs['pool_s'].append(pool_s)
        outs['conv_s'].append(conv_s)
    wkv_s_all = outs.pop('wkv_s')[-1]
    st = {k: jnp.stack(v) for k, v in outs.items()}
    return (xp.reshape(batch, seq, D_MODEL), xs.reshape(nsamp, 1, D_MODEL), st['shift_p'], st['wkv_p'], st['pool_p'],
            st['conv_p'], st['shift_s'], wkv_s_all, st['pool_s'], st['conv_s'])


def kernel(x_prompt, x_sample, state_shift, state_wkv, state_pool, state_conv, c_prompt, c_sample, ada_w, ada_b, w_in, mu_shift, decay_w0, decay_up, aaa_a0, aaa_up, gate_up, k_k, k_a, r_k, gn_w, gn_b, w_branch_rwkv, pool_w, pool_scale, w_branch_pool, w_out, ln1_w, ln1_b, ffn_up, conv_w, conv_b, ffn_down, ln2_w, ln2_b):
    p = {'w_in': w_in, 'mu_shift': mu_shift, 'decay_w0': decay_w0, 'decay_up': decay_up, 'aaa_a0': aaa_a0,
         'aaa_up': aaa_up, 'gate_up': gate_up, 'k_k': k_k, 'k_a': k_a, 'r_k': r_k, 'gn_w': gn_w, 'gn_b': gn_b,
         'w_branch_rwkv': w_branch_rwkv, 'pool_w': pool_w, 'pool_scale': pool_scale,
         'w_branch_pool': w_branch_pool, 'w_out': w_out, 'ln1_w': ln1_w, 'ln1_b': ln1_b, 'ffn_up': ffn_up,
         'conv_w': conv_w, 'conv_b': conv_b, 'ffn_down': ffn_down, 'ln2_w': ln2_w, 'ln2_b': ln2_b}
    c_all = jnp.concatenate([c_prompt, c_sample], axis=0).astype(F32)
    mod = _modulation(c_all, ada_w, ada_b)
    dt = x_prompt.dtype
    out = _trunk(x_prompt, x_sample, state_shift, state_wkv, state_pool, state_conv, mod, p)
    return tuple(o.astype(dt) for o in out)
```

```python
import functools
import math

import jax
import jax.numpy as jnp
from jax import lax
from jax.experimental import pallas as pl
from jax.experimental.pallas import tpu as pltpu

F32 = jnp.float32
BF16 = jnp.bfloat16

D_MODEL = 1024
HEAD_SIZE = 64
N_HEADS = D_MODEL // HEAD_SIZE
D_RWKV = N_HEADS * HEAD_SIZE
D_DECAY_LORA = 64
D_AAA_LORA = 64
D_GATE_LORA = 128
D_LORA = D_DECAY_LORA + D_AAA_LORA + D_GATE_LORA
POOL_WINDOWS = (2, 4, 8, 16)
POOL_GROUP = D_MODEL // 8
D_POOL = len(POOL_WINDOWS) * POOL_GROUP
POOL_BUF = max(POOL_WINDOWS) - 1
D_FF = (D_MODEL * 11) // 4
CONV_W = 3
D_SHIFT = 3 * D_RWKV + D_LORA
IN_COLS = D_SHIFT + D_POOL + 2 * D_MODEL
LN_EPS = 1e-5
GN_EPS = 64e-5
EXP_M05 = math.exp(-0.5)

VMEM_LIMIT_BYTES = 56 * 1024 * 1024
ROW_TILE = 256
CHUNK = 64
WKV_CHUNKS_PER_STEP = 4
SAMPLE_VALUE_UNROLL = 4
FF_CHUNK = 1408


def _ln(x):
    mu = jnp.mean(x, axis=-1, keepdims=True)
    xc = x - mu
    var = jnp.mean(xc * xc, axis=-1, keepdims=True)
    return xc * lax.rsqrt(var + LN_EPS)


def _dot(a, b):
    return jnp.dot(a.astype(BF16), b.astype(BF16), preferred_element_type=F32)


def _dot_nt(a, b):
    return lax.dot_general(a.astype(BF16), b.astype(BF16), (((1,), (1,)), ((), ())),
                           preferred_element_type=F32)


def _dot_tn(a, b):
    return lax.dot_general(a.astype(BF16), b.astype(BF16), (((0,), (0,)), ((), ())),
                           preferred_element_type=F32)


def _const_spec(shape):
    nd = len(shape)
    return pl.BlockSpec(shape, lambda *_: (0,) * nd, pipeline_mode=pl.Buffered(1))


def _layer_spec(shape, l):
    nd = len(shape)
    return pl.BlockSpec((None,) + tuple(shape), lambda *_: (l,) + (0,) * nd, pipeline_mode=pl.Buffered(1))


def _params(sem):
    return pltpu.CompilerParams(dimension_semantics=sem, vmem_limit_bytes=VMEM_LIMIT_BYTES)


def _mod_kernel(c_ref, w_ref, b_ref, o_ref):
    o_ref[...] = _dot(c_ref[...], w_ref[...]) + b_ref[...]


def _modulation(c_all, ada_w, ada_b):
    depth, d, n = ada_w.shape
    rows = c_all.shape[0]
    tn = 1536
    return pl.pallas_call(
        _mod_kernel,
        grid=(depth, n // tn),
        in_specs=[pl.BlockSpec((rows, d), lambda l, j: (0, 0)),
                  pl.BlockSpec((None, d, tn), lambda l, j: (l, 0, j)),
                  pl.BlockSpec((None, 1, tn), lambda l, j: (l, 0, j))],
        out_specs=pl.BlockSpec((None, rows, tn), lambda l, j: (l, 0, j)),
        out_shape=jax.ShapeDtypeStruct((depth, rows, n), F32),
        compiler_params=_params(("arbitrary", "arbitrary")),
    )(c_all, ada_w, ada_b.reshape(depth, 1, n))


def _head_sum(x, mpair):
    rows = x.shape[0]
    tiles = range(x.shape[1] // (2 * HEAD_SIZE))
    st = jnp.concatenate([x[:, t * 2 * HEAD_SIZE:(t + 1) * 2 * HEAD_SIZE] for t in tiles], axis=0)
    sm = _dot(st, mpair)
    return jnp.concatenate([sm[t * rows:(t + 1) * rows] for t in tiles], axis=1)


def _rwkv_prep_dots(xs_k, lora, wl_ref, kk_ref, mpair_ref):
    kk = xs_k * kk_ref[...]
    ss = _head_sum(kk * kk, mpair_ref[...])
    lane = lax.broadcasted_iota(jnp.int32, lora.shape, 1)
    act = jnp.where(lane < D_DECAY_LORA, jnp.tanh(lora),
                    jnp.where(lane < D_DECAY_LORA + D_AAA_LORA, lora, jax.nn.sigmoid(lora)))
    z = _dot(act, wl_ref[...])
    return z, kk, ss


def _rwkv_prep_finish(xs_r, xs_k, xs_v, z, kk, ss, w0_ref, a0_ref, ka_ref):
    dec = w0_ref[...] + z[:, 0:D_RWKV]
    lw = -jax.nn.sigmoid(dec) * EXP_M05
    a = jax.nn.sigmoid(a0_ref[...] + z[:, D_RWKV:2 * D_RWKV])
    g = z[:, 2 * D_RWKV:3 * D_RWKV]
    kap = kk / jnp.maximum(jnp.sqrt(ss), 1e-12)
    kmod = xs_k * (1.0 + (a - 1.0) * ka_ref[...])
    return xs_r, kmod, xs_v, kap, kap * a, lw, g


def _pool_mix(pooled, p_in, poolw_ref, pscale_ref, wbp_ref):
    mixed = []
    for gi in range(len(POOL_WINDOWS)):
        sl = slice(gi * POOL_GROUP, (gi + 1) * POOL_GROUP)
        diff = pooled[gi] - p_in[:, sl]
        mixed.append(_dot(diff, poolw_ref[gi]) * pscale_ref[:, sl])
    return _dot(jnp.concatenate(mixed, axis=-1), wbp_ref[...])


def _mix_in_prompt_kernel(x_ref, mod_ref, win_ref, mu_ref, wl_ref, w0_ref, a0_ref, kk_ref, ka_ref,
                          mpair_ref, poolw_ref, pscale_ref, wbp_ref,
                          r_ref, k_ref, v_ref, kap_ref, b_ref, lw_ref, g_ref, gpool_ref, sgr_ref,
                          shift_ref, pool_ref,
                          carry_ref, pext_ref, *, tm):
    j = pl.program_id(1)

    @pl.when(j == 0)
    def _():
        carry_ref[...] = jnp.zeros(carry_ref.shape, F32)
        pext_ref[...] = jnp.zeros(pext_ref.shape, F32)

    x = x_ref[...]
    h = _ln(x) * (1.0 + mod_ref[:, D_MODEL:2 * D_MODEL]) + mod_ref[:, 0:D_MODEL]
    shift_ref[...] = h[tm - 1:tm, :]
    hb = h.astype(BF16)
    row = lax.broadcasted_iota(jnp.int32, (tm, 1), 0)

    def shifted(c0, c1):
        cur = jnp.dot(hb, win_ref[:, c0:c1], preferred_element_type=F32)
        prev = jnp.where(row == 0, carry_ref[7:8, c0:c1], pltpu.roll(cur, 1, 0))
        carry_ref[7:8, c0:c1] = cur[tm - 1:tm, :]
        return cur + (prev - cur) * mu_ref[:, c0:c1]

    xs_k = shifted(D_RWKV, 2 * D_RWKV)
    lora = shifted(3 * D_RWKV, D_SHIFT)
    z, kk, ss = _rwkv_prep_dots(xs_k, lora, wl_ref, kk_ref, mpair_ref)
    p_in = jnp.dot(hb, win_ref[:, D_SHIFT:D_SHIFT + D_POOL], preferred_element_type=F32)
    gates = jnp.dot(hb, win_ref[:, D_SHIFT + D_POOL:IN_COLS], preferred_element_type=F32)

    pext_ref[0:16, :] = pext_ref[tm:tm + 16, :]
    pext_ref[16:16 + tm, :] = p_in
    pos1 = (j * tm + row + 1).astype(F32)
    pooled = []
    for gi, win in enumerate(POOL_WINDOWS):
        sl = slice(gi * POOL_GROUP, (gi + 1) * POOL_GROUP)
        acc = p_in[:, sl]
        for d in range(1, win):
            acc = acc + pext_ref[16 - d:16 - d + tm, sl]
        pooled.append(acc / jnp.minimum(float(win), pos1))
    pool_ref[...] = pext_ref[tm:tm + 16, :]
    y_pool = _pool_mix(pooled, p_in, poolw_ref, pscale_ref, wbp_ref)

    xs_r = shifted(0, D_RWKV)
    xs_v = shifted(2 * D_RWKV, 3 * D_RWKV)
    r, kmod, v, kap, b, lw, g = _rwkv_prep_finish(xs_r, xs_k, xs_v, z, kk, ss, w0_ref, a0_ref, ka_ref)
    r_ref[...] = r
    k_ref[...] = kmod
    v_ref[...] = v
    kap_ref[...] = kap
    b_ref[...] = b
    lw_ref[...] = lw
    g_ref[...] = g.astype(BF16)
    sgr_ref[...] = jax.nn.sigmoid(gates[:, D_MODEL:2 * D_MODEL]).astype(BF16)
    gpool_ref[...] = (jax.nn.sigmoid(gates[:, 0:D_MODEL]) * y_pool).astype(BF16)


def _mix_in_sample_kernel(x_ref, mod_ref, sprev_ref, pst_ref, win_ref, mu_ref, wl_ref, w0_ref, a0_ref, kk_ref,
                          ka_ref, mpair_ref, poolw_ref, pscale_ref, wbp_ref,
                          r_ref, k_ref, v_ref, kap_ref, b_ref, w_ref, g_ref, gpool_ref, sgr_ref,
                          shift_ref, pool_ref):
    x = x_ref[...]
    h = _ln(x) * (1.0 + mod_ref[:, D_MODEL:2 * D_MODEL]) + mod_ref[:, 0:D_MODEL]
    shift_ref[...] = h
    hb = h.astype(BF16)
    pb = sprev_ref[...].astype(BF16)

    def shifted(c0, c1):
        w = win_ref[:, c0:c1]
        cur = jnp.dot(hb, w, preferred_element_type=F32)
        prev = jnp.dot(pb, w, preferred_element_type=F32)
        return cur + (prev - cur) * mu_ref[:, c0:c1]

    xs_r = shifted(0, D_RWKV)
    xs_k = shifted(D_RWKV, 2 * D_RWKV)
    xs_v = shifted(2 * D_RWKV, 3 * D_RWKV)
    lora = shifted(3 * D_RWKV, D_SHIFT)
    z, kk, ss = _rwkv_prep_dots(xs_k, lora, wl_ref, kk_ref, mpair_ref)
    r, kmod, v, kap, b, lw, g = _rwkv_prep_finish(xs_r, xs_k, xs_v, z, kk, ss, w0_ref, a0_ref, ka_ref)
    r_ref[...] = r.T
    k_ref[...] = kmod.T
    v_ref[...] = v.T
    kap_ref[...] = kap.T
    b_ref[...] = b.T
    w_ref[...] = jnp.exp(lw).T
    g_ref[...] = g.T

    p_in = jnp.dot(hb, win_ref[:, D_SHIFT:D_SHIFT + D_POOL], preferred_element_type=F32)
    pooled = []
    for gi, win in enumerate(POOL_WINDOWS):
        sl = slice(gi * POOL_GROUP, (gi + 1) * POOL_GROUP)
        acc = p_in[:, sl]
        for d in range(1, win):
            acc = acc + pst_ref[:, POOL_BUF - d, sl]
        pooled.append(acc / float(win))
    gates = jnp.dot(hb, win_ref[:, D_SHIFT + D_POOL:IN_COLS], preferred_element_type=F32)
    gpool = jax.nn.sigmoid(gates[:, 0:D_MODEL]) * _pool_mix(pooled, p_in, poolw_ref, pscale_ref, wbp_ref)
    gpool_ref[...] = gpool.astype(BF16)
    sgr_ref[...] = jax.nn.sigmoid(gates[:, D_MODEL:2 * D_MODEL]).astype(BF16)
    for i in range(POOL_BUF - 1):
        pool_ref[:, i, :] = pst_ref[:, i + 1, :]
    pool_ref[:, POOL_BUF - 1, :] = p_in


def _mix_in_weight_specs(l):
    return [_layer_spec((D_MODEL, IN_COLS), l),
            _layer_spec((1, D_SHIFT), l),
            _layer_spec((D_LORA, 3 * D_RWKV), l),
            _layer_spec((1, D_RWKV), l),
            _layer_spec((1, D_RWKV), l),
            _layer_spec((1, D_RWKV), l),
            _layer_spec((1, D_RWKV), l),
            _const_spec((2 * HEAD_SIZE, 2 * HEAD_SIZE)),
            _layer_spec((len(POOL_WINDOWS), POOL_GROUP, POOL_GROUP), l),
            _layer_spec((1, D_POOL), l),
            _layer_spec((D_POOL, D_MODEL), l)]


def _mix_in_prompt(x2d, mod3, wts, l, batch, seq):
    tm = min(ROW_TILE, seq)
    nj = seq // tm
    rows = batch * seq
    row_spec = pl.BlockSpec((tm, D_MODEL), lambda b, j: (b * nj + j, 0))
    f32_rows = jax.ShapeDtypeStruct((rows, D_MODEL), F32)
    bf_rows = jax.ShapeDtypeStruct((rows, D_MODEL), BF16)
    return pl.pallas_call(
        functools.partial(_mix_in_prompt_kernel, tm=tm),
        grid=(batch, nj),
        in_specs=[row_spec, pl.BlockSpec((None, 1, 6 * D_MODEL), lambda b, j: (b, 0, 0))] + _mix_in_weight_specs(l),
        out_specs=[row_spec] * 9 + [pl.BlockSpec((None, 1, D_MODEL), lambda b, j: (b, 0, 0)),
                                    pl.BlockSpec((None, 16, D_POOL), lambda b, j: (b, 0, 0))],
        out_shape=[f32_rows] * 6 + [bf_rows] * 3 + [jax.ShapeDtypeStruct((batch, 1, D_MODEL), F32),
                                                    jax.ShapeDtypeStruct((batch, 16, D_POOL), F32)],
        scratch_shapes=[pltpu.VMEM((8, D_SHIFT), F32), pltpu.VMEM((tm + 16, D_POOL), F32)],
        compiler_params=_params(("arbitrary", "arbitrary")),
    )(x2d, mod3, *wts)


def _mix_in_sample(x2d, mod2, shift_all, pool_all, wts, l):
    n = x2d.shape[0]
    out_full = lambda shape: pl.BlockSpec(shape, lambda i: (0,) * len(shape))
    f32_rows = jax.ShapeDtypeStruct((n, D_MODEL), F32)
    f32_cols = jax.ShapeDtypeStruct((D_RWKV, n), F32)
    bf_rows = jax.ShapeDtypeStruct((n, D_MODEL), BF16)
    return pl.pallas_call(
        _mix_in_sample_kernel,
        grid=(1,),
        in_specs=[_const_spec((n, D_MODEL)), _const_spec((n, 6 * D_MODEL)), _layer_spec((n, D_MODEL), l),
                  _layer_spec((n, POOL_BUF, D_POOL), l)] + _mix_in_weight_specs(l),
        out_specs=[out_full((D_RWKV, n))] * 7 + [out_full((n, D_MODEL))] * 3 + [out_full((n, POOL_BUF, D_POOL))],
        out_shape=[f32_cols] * 7 + [bf_rows] * 2 + [f32_rows, jax.ShapeDtypeStruct((n, POOL_BUF, D_POOL), F32)],
        compiler_params=_params(("arbitrary",)),
    )(x2d, mod2, shift_all, pool_all, *wts)


def _wkv_prompt_kernel(r_ref, k_ref, v_ref, kap_ref, b_ref, lw_ref, g_ref, rk_ref, gnw_ref, gnb_ref, mpair_ref,
                       yg_ref, snew_ref, s_ref, *, nch):
    @pl.when(pl.program_id(1) == 0)
    def _():
        s_ref[...] = jnp.zeros(s_ref.shape, F32)

    L = CHUNK
    pw2 = 2 * HEAD_SIZE
    lane = lax.broadcasted_iota(jnp.int32, (L, pw2), 1)
    trow = lax.broadcasted_iota(jnp.int32, (L, pw2), 0)
    left = lane < HEAD_SIZE
    src = jnp.where(left, lane, lane - HEAD_SIZE)
    strict = src < trow
    incl = src <= trow
    eye = (src == trow).astype(F32)
    bd_mask = ((lax.broadcasted_iota(jnp.int32, (pw2, pw2), 0) < HEAD_SIZE)
               == (lax.broadcasted_iota(jnp.int32, (pw2, pw2), 1) < HEAD_SIZE))

    crow = lax.broadcasted_iota(jnp.int32, (L, 1), 0)

    def bd(x):
        xb = x.astype(BF16)
        zero = jnp.zeros_like(xb)
        return jnp.concatenate([jnp.where(left, xb, zero), jnp.where(left, zero, xb)], axis=0)

    def cumsum_rows(x):
        sh = 1
        while sh < L:
            x = x + jnp.where(crow >= sh, pltpu.roll(x, sh, 0), 0.0)
            sh *= 2
        return x

    chunks = range(nch)
    pairs = range(N_HEADS // 2)
    cs = [slice(p * pw2, (p + 1) * pw2) for p in pairs]
    units = [(j, p) for j in chunks for p in pairs]

    r, kmod, v, g, dec_last, a_hat, r_hat, b_hat, k_hat, b_til, k_til = ({} for _ in range(11))
    for j in chunks:
        rs = slice(j * L, (j + 1) * L)
        lw = lw_ref[rs, :]
        p_inc = cumsum_rows(lw)
        p_last = p_inc[L - 1:L, :]
        e_p = jnp.exp(p_inc)
        e_np = jnp.exp(-p_inc)
        e_l = jnp.exp(p_last - p_inc)
        dec_last[j] = jnp.exp(p_last)
        r[j], kmod[j], v[j], bb = r_ref[rs, :], k_ref[rs, :], v_ref[rs, :], b_ref[rs, :]
        g[j] = g_ref[rs, :].astype(F32)
        r_hat[j] = r[j] * e_p
        a_hat[j] = -kap_ref[rs, :] * jnp.exp(p_inc - lw)
        b_hat[j] = bb * e_np
        k_hat[j] = kmod[j] * e_np
        b_til[j] = bb * e_l
        k_til[j] = kmod[j] * e_l

    ar = {(j, p): jnp.concatenate([a_hat[j][:, cs[p]], r_hat[j][:, cs[p]]], axis=0).astype(BF16) for j, p in units}
    gram = {(j, p): _dot_nt(ar[j, p], jnp.concatenate([bd(b_hat[j][:, cs[p]]), bd(k_hat[j][:, cs[p]])], axis=0))
            for j, p in units}
    nab = {u_: jnp.where(strict, gram[u_][0:L, 0:pw2], 0.0) for u_ in units}
    nak = {u_: jnp.where(strict, gram[u_][0:L, pw2:2 * pw2], 0.0) for u_ in units}
    rbk = {u_: jnp.concatenate([jnp.where(incl, gram[u_][L:2 * L, 0:pw2], 0.0),
                                jnp.where(incl, gram[u_][L:2 * L, pw2:2 * pw2], 0.0)], axis=1).astype(BF16)
           for u_ in units}
    inv = {u_: eye + nab[u_] for u_ in units}
    pw = {u_: _dot(nab[u_], bd(nab[u_])) for u_ in units}
    for _ in range(4):
        px = {u_: _dot(jnp.concatenate([pw[u_], inv[u_]], axis=0), bd(pw[u_])) for u_ in units}
        pw = {u_: px[u_][0:L] for u_ in units}
        inv = {u_: inv[u_] + px[u_][L:2 * L] for u_ in units}
    inv = {u_: inv[u_] + _dot(inv[u_], bd(pw[u_])) for u_ in units}
    vp = {(j, p): v[j][:, cs[p]] for j, p in units}
    bdv = {u_: bd(vp[u_]) for u_ in units}
    akv = {u_: _dot(nak[u_], bdv[u_]) for u_ in units}

    s_cur = [s_ref[p] for p in pairs]
    y = {}
    for j in chunks:
        ars = [_dot_nt(ar[j, p], s_cur[p]) for p in pairs]
        u = [_dot(inv[j, p], bd(ars[p][0:L] + akv[j, p])) for p in pairs]
        for p in pairs:
            y[j, p] = ars[p][L:2 * L] + _dot(rbk[j, p], jnp.concatenate([bd(u[p]), bdv[j, p]], axis=0))
        upd = [_dot_tn(jnp.concatenate([u[p], vp[j, p]], axis=0),
                       jnp.concatenate([b_til[j][:, cs[p]], k_til[j][:, cs[p]]], axis=0)) for p in pairs]
        s_cur = [s_cur[p] * dec_last[j][:, cs[p]] + jnp.where(bd_mask, upd[p], 0.0) for p in pairs]
    for p in pairs:
        s_ref[p] = s_cur[p]
        snew_ref[p] = s_cur[p]

    inv_n = 1.0 / HEAD_SIZE
    mpair = mpair_ref[...]
    for j in chunks:
        y_all = jnp.concatenate([y[j, p] for p in pairs], axis=1)
        mu = _head_sum(y_all, mpair) * inv_n
        yc = y_all - mu
        var = _head_sum(yc * yc, mpair) * inv_n
        yn = yc * lax.rsqrt(var + GN_EPS) * gnw_ref[...] + gnb_ref[...]
        bonus = _head_sum(r[j] * kmod[j] * rk_ref[...], mpair) * v[j]
        yg_ref[j * L:(j + 1) * L, :] = ((yn + bonus) * g[j]).astype(BF16)


def _wkv_prompt(ops, rk, gnw, gnb, mpair, l, batch, seq):
    nch = math.gcd(seq // CHUNK, WKV_CHUNKS_PER_STEP)
    rows_per_step = nch * CHUNK
    nc = seq // rows_per_step
    rows = batch * seq
    npair = N_HEADS // 2
    pw2 = 2 * HEAD_SIZE
    row_spec = pl.BlockSpec((rows_per_step, D_RWKV), lambda b, c: (b * nc + c, 0))
    yg, sbd = pl.pallas_call(
        functools.partial(_wkv_prompt_kernel, nch=nch),
        grid=(batch, nc),
        in_specs=[row_spec] * 7 + [_layer_spec((1, D_RWKV), l)] * 3 + [_const_spec((pw2, pw2))],
        out_specs=[row_spec, pl.BlockSpec((None, npair, pw2, pw2), lambda b, c: (b, 0, 0, 0))],
        out_shape=[jax.ShapeDtypeStruct((rows, D_RWKV), BF16),
                   jax.ShapeDtypeStruct((batch, npair, pw2, pw2), F32)],
        scratch_shapes=[pltpu.VMEM((npair, pw2, pw2), F32)],
        compiler_params=_params(("arbitrary", "arbitrary")),
    )(*ops, rk, gnw, gnb, mpair)
    n = HEAD_SIZE
    wkv = jnp.stack([sbd[:, :, 0:n, 0:n], sbd[:, :, n:2 * n, n:2 * n]], axis=2)
    return yg, wkv.reshape(batch, N_HEADS, n, n)


def _wkv_sample_kernel(r_ref, k_ref, v_ref, kap_ref, b_ref, w_ref, g_ref, rk_ref, gnw_ref, gnb_ref, s_ref, *rest):
    *prev_refs, yg_ref, snew_ref, y_ref = rest
    slot = len(prev_refs)
    for j, prev_ref in enumerate(prev_refs):
        snew_ref[j] = prev_ref[...]
    kap, w, nb, kmod, r = kap_ref[...], w_ref[...], -b_ref[...], k_ref[...], r_ref[...]

    def body(vi, carry):
        row = pl.ds(vi, 1)
        s = s_ref[vi]
        sa = jnp.sum(s * kap, axis=0, keepdims=True)
        s_new = s * w + sa * nb + v_ref[row, :] * kmod
        snew_ref[slot, vi] = s_new
        y_ref[row, :] = jnp.sum(s_new * r, axis=0, keepdims=True)
        return carry

    lax.fori_loop(0, HEAD_SIZE, body, 0, unroll=SAMPLE_VALUE_UNROLL)

    y = y_ref[...]
    mu = jnp.mean(y, axis=0, keepdims=True)
    yc = y - mu
    var = jnp.mean(yc * yc, axis=0, keepdims=True)
    yn = yc * lax.rsqrt(var + GN_EPS) * gnw_ref[...] + gnb_ref[...]
    bonus = jnp.sum(r * kmod * rk_ref[...], axis=0, keepdims=True) * v_ref[...]
    yg_ref[...] = (yn + bonus) * g_ref[...]


def _wkv_sample(ops, rk, gnw, gnb, state_all, prev_new, l):
    depth, n = state_all.shape[0], state_all.shape[-1]
    prev = list(prev_new) if l == depth - 1 else []
    slots = len(prev) + 1
    hn = HEAD_SIZE
    col_spec = pl.BlockSpec((hn, n), lambda h: (h, 0))
    par_spec = pl.BlockSpec((None, hn, 1), lambda h: (l, h, 0))
    tile = (hn, hn, n)
    st_in = pl.BlockSpec((None, None) + tile, lambda h: (l, h, 0, 0, 0))
    prev_spec = pl.BlockSpec((None, None) + tile, lambda h: (0, h, 0, 0, 0))
    st_out = pl.BlockSpec((slots, None) + tile, lambda h: (0, h, 0, 0, 0))
    return pl.pallas_call(
        _wkv_sample_kernel,
        grid=(N_HEADS,),
        in_specs=[col_spec] * 7 + [par_spec] * 3 + [st_in] + [prev_spec] * len(prev),
        out_specs=[col_spec, st_out],
        out_shape=[jax.ShapeDtypeStruct((D_RWKV, n), F32),
                   jax.ShapeDtypeStruct((slots, N_HEADS) + tile, F32)],
        scratch_shapes=[pltpu.VMEM((hn, n), F32)],
        compiler_params=_params(("arbitrary",)),
    )(*ops, rk, gnw, gnb, state_all, *prev)


def _mix_out_kernel(yg_ref, gpool_ref, sgr_ref, x_ref, mod_ref, wbr_ref, wout_ref, lnw_ref, lnb_ref, o_ref, *, alpha,
                    yg_transposed):
    yg = yg_ref[...].T if yg_transposed else yg_ref[...]
    y_rwkv = _dot(yg, wbr_ref[...])
    merged = gpool_ref[...].astype(F32) + sgr_ref[...].astype(F32) * y_rwkv
    mix = _dot(merged, wout_ref[...])
    g1 = mod_ref[:, 2 * D_MODEL:3 * D_MODEL]
    o_ref[...] = _ln(alpha * x_ref[...] + (1.0 + g1) * mix) * lnw_ref[...] + lnb_ref[...]


def _mix_out(yg, gpool, sgr, x2d, mod, mod_spec, grid, row_map, tm, wts, l, alpha, yg_transposed=False):
    row_spec = pl.BlockSpec((tm, D_MODEL), row_map)
    yg_spec = _const_spec(yg.shape) if yg_transposed else row_spec
    return pl.pallas_call(
        functools.partial(_mix_out_kernel, alpha=alpha, yg_transposed=yg_transposed),
        grid=grid,
        in_specs=[yg_spec] + [row_spec] * 3
        + [mod_spec, _layer_spec((D_RWKV, D_MODEL), l), _layer_spec((D_MODEL, D_MODEL), l),
           _layer_spec((1, D_MODEL), l), _layer_spec((1, D_MODEL), l)],
        out_specs=row_spec,
        out_shape=jax.ShapeDtypeStruct(x2d.shape, F32),
        compiler_params=_params(("arbitrary",) * len(grid)),
    )(yg, gpool, sgr, x2d, mod, *wts)


def _ffn_prompt_kernel(x_ref, mod_ref, up_ref, cw_ref, cb_ref, down_ref, lnw_ref, lnb_ref,
                       o_ref, conv_ref, carry_ref, *, tm, alpha):
    j = pl.program_id(1)
    first = j == 0
    x = x_ref[...]
    h2 = _ln(x) * (1.0 + mod_ref[:, 4 * D_MODEL:5 * D_MODEL]) + mod_ref[:, 3 * D_MODEL:4 * D_MODEL]
    hb = h2.astype(BF16)
    row = lax.broadcasted_iota(jnp.int32, (tm, 1), 0)

    @pl.when(first)
    def _():
        carry_ref[...] = jnp.zeros(carry_ref.shape, F32)

    def conv(c0, c1):
        u = jnp.dot(hb, up_ref[:, c0:c1], preferred_element_type=F32)
        c6 = carry_ref[6:7, c0:c1]
        c7 = carry_ref[7:8, c0:c1]
        p1 = jnp.where(row == 0, c7, pltpu.roll(u, 1, 0))
        p2 = jnp.where(row == 0, c6, jnp.where(row == 1, c7, pltpu.roll(u, 2, 0)))
        carry_ref[6:8, c0:c1] = u[tm - 2:tm, :]
        conv_ref[:, c0:c1] = u[tm - 2:tm, :]
        return cb_ref[:, c0:c1] + p2 * cw_ref[0:1, c0:c1] + p1 * cw_ref[1:2, c0:c1] + u * cw_ref[2:3, c0:c1]

    nchunk = D_FF // FF_CHUNK
    val = [conv(c * FF_CHUNK, (c + 1) * FF_CHUNK) for c in range(nchunk)]
    gate = [conv(D_FF + c * FF_CHUNK, D_FF + (c + 1) * FF_CHUNK) for c in range(nchunk)]
    ff = jnp.zeros((tm, D_MODEL), F32)
    for c in range(nchunk):
        ff = ff + _dot(val[c] * jax.nn.silu(gate[c]), down_ref[c * FF_CHUNK:(c + 1) * FF_CHUNK, :])
    g2 = mod_ref[:, 5 * D_MODEL:6 * D_MODEL]
    o_ref[...] = _ln(alpha * x + (1.0 + g2) * ff) * lnw_ref[...] + lnb_ref[...]


def _ffn_weight_specs(l):
    return [_layer_spec((D_MODEL, 2 * D_FF), l), _layer_spec((CONV_W, 2 * D_FF), l), _layer_spec((1, 2 * D_FF), l),
            _layer_spec((D_FF, D_MODEL), l), _layer_spec((1, D_MODEL), l), _layer_spec((1, D_MODEL), l)]


def _ffn_prompt(x2d, mod3, wts, l, batch, seq, alpha):
    tm = min(ROW_TILE, seq)
    nj = seq // tm
    row_spec = pl.BlockSpec((tm, D_MODEL), lambda b, j: (b * nj + j, 0))
    return pl.pallas_call(
        functools.partial(_ffn_prompt_kernel, tm=tm, alpha=alpha),
        grid=(batch, nj),
        in_specs=[row_spec, pl.BlockSpec((None, 1, 6 * D_MODEL), lambda b, j: (b, 0, 0))] + _ffn_weight_specs(l),
        out_specs=[row_spec, pl.BlockSpec((None, CONV_W - 1, 2 * D_FF), lambda b, j: (b, 0, 0))],
        out_shape=[jax.ShapeDtypeStruct(x2d.shape, F32),
                   jax.ShapeDtypeStruct((batch, CONV_W - 1, 2 * D_FF), F32)],
        scratch_shapes=[pltpu.VMEM((8, 2 * D_FF), F32)],
        compiler_params=_params(("arbitrary", "arbitrary")),
    )(x2d, mod3, *wts)


def _ffn_sample_kernel(x_ref, mod_ref, cst_ref, up_ref, cw_ref, cb_ref, down_ref, lnw_ref, lnb_ref,
                       o_ref, conv_ref, *, alpha):
    x = x_ref[...]
    h2 = _ln(x) * (1.0 + mod_ref[:, 4 * D_MODEL:5 * D_MODEL]) + mod_ref[:, 3 * D_MODEL:4 * D_MODEL]
    hb = h2.astype(BF16)

    def conv(c0, c1):
        u = jnp.dot(hb, up_ref[:, c0:c1], preferred_element_type=F32)
        p2 = cst_ref[:, 0, c0:c1]
        p1 = cst_ref[:, 1, c0:c1]
        conv_ref[:, 0, c0:c1] = p1
        conv_ref[:, 1, c0:c1] = u
        return cb_ref[:, c0:c1] + p2 * cw_ref[0:1, c0:c1] + p1 * cw_ref[1:2, c0:c1] + u * cw_ref[2:3, c0:c1]

    ff = jnp.zeros(x.shape, F32)
    for c in range(D_FF // FF_CHUNK):
        val = conv(c * FF_CHUNK, (c + 1) * FF_CHUNK)
        gate = conv(D_FF + c * FF_CHUNK, D_FF + (c + 1) * FF_CHUNK)
        ff = ff + _dot(val * jax.nn.silu(gate), down_ref[c * FF_CHUNK:(c + 1) * FF_CHUNK, :])
    g2 = mod_ref[:, 5 * D_MODEL:6 * D_MODEL]
    o_ref[...] = _ln(alpha * x + (1.0 + g2) * ff) * lnw_ref[...] + lnb_ref[...]


def _ffn_sample(x2d, mod2, conv_all, wts, l, alpha):
    n = x2d.shape[0]
    st = (n, CONV_W - 1, 2 * D_FF)
    out_full = lambda shape: pl.BlockSpec(shape, lambda i: (0,) * len(shape))
    return pl.pallas_call(
        functools.partial(_ffn_sample_kernel, alpha=alpha),
        grid=(1,),
        in_specs=[_const_spec((n, D_MODEL)), _const_spec((n, 6 * D_MODEL)), _layer_spec(st, l)]
        + _ffn_weight_specs(l),
        out_specs=[out_full((n, D_MODEL)), out_full(st)],
        out_shape=[jax.ShapeDtypeStruct(x2d.shape, F32), jax.ShapeDtypeStruct(st, F32)],
        compiler_params=_params(("arbitrary",)),
    )(x2d, mod2, conv_all, *wts)


def _stacked_weights(p):
    depth = p['w_in'].shape[0]
    row = lambda a: a.reshape(depth, 1, -1)
    bf = lambda a: a.astype(BF16)
    zeros = lambda r, c: jnp.zeros((depth, r, c), F32)
    lora_up = jnp.concatenate([
        jnp.concatenate([p['decay_up'], zeros(D_DECAY_LORA, 2 * D_RWKV)], axis=2),
        jnp.concatenate([zeros(D_AAA_LORA, D_RWKV), p['aaa_up'], zeros(D_AAA_LORA, D_RWKV)], axis=2),
        jnp.concatenate([zeros(D_GATE_LORA, 2 * D_RWKV), p['gate_up']], axis=2)], axis=1)
    head = jnp.arange(2 * HEAD_SIZE) // HEAD_SIZE
    mpair = (head[:, None] == head[None, :]).astype(BF16)
    mix_in = (bf(p['w_in']), row(p['mu_shift']), bf(lora_up), row(p['decay_w0']), row(p['aaa_a0']),
              row(p['k_k']), row(p['k_a']), mpair, bf(p['pool_w']), row(p['pool_scale']), bf(p['w_branch_pool']))
    wkv = (row(p['r_k']), row(p['gn_w']), row(p['gn_b']))
    col = lambda a: a.reshape(depth, -1, 1)
    wkv_cols = (col(p['r_k']), col(p['gn_w']), col(p['gn_b']))
    mix_out = (bf(p['w_branch_rwkv']), bf(p['w_out']), row(p['ln1_w']), row(p['ln1_b']))
    ffn = (bf(p['ffn_up']), p['conv_w'], row(p['conv_b']), bf(p['ffn_down']), row(p['ln2_w']), row(p['ln2_b']))
    return mix_in, wkv, wkv_cols, mix_out, ffn


def _trunk(x_prompt, x_sample, state_shift, state_wkv, state_pool, state_conv, mod, p):
    depth = p['w_in'].shape[0]
    alpha = float((2 * depth) ** 0.25)
    batch, seq, _ = x_prompt.shape
    nsamp = x_sample.shape[0]
    xp = x_prompt.reshape(batch * seq, D_MODEL).astype(F32)
    xs = x_sample.reshape(nsamp, D_MODEL).astype(F32)
    tm = min(ROW_TILE, seq)
    nj = seq // tm
    prompt_mod_spec = pl.BlockSpec((None, 1, 6 * D_MODEL), lambda i: (i // nj, 0, 0))
    sample_mod_spec = pl.BlockSpec((nsamp, 6 * D_MODEL), lambda i: (0, 0))
    w_mix_in, w_wkv, w_wkv_cols, w_mix_out, w_ffn = _stacked_weights(p)
    shift_all = state_shift.astype(F32)
    wkv_all = jnp.transpose(state_wkv.astype(F32), (0, 2, 3, 4, 1))
    pool_all = state_pool.astype(F32)
    conv_all = state_conv.astype(F32)
    outs = {k: [] for k in ('shift_p', 'wkv_p', 'pool_p', 'conv_p', 'shift_s', 'wkv_s', 'pool_s', 'conv_s')}
    for l in range(depth):
        mod_p = mod[l, 0:batch].reshape(batch, 1, 6 * D_MODEL)
        mod_s = mod[l, batch:batch + nsamp]

        *ops, gpool, sgr, shift_p, pool_p = _mix_in_prompt(xp, mod_p, w_mix_in, l, batch, seq)
        yg, wkv_p = _wkv_prompt(ops, *w_wkv, w_mix_in[7], l, batch, seq)
        xp = _mix_out(yg, gpool, sgr, xp, mod_p, prompt_mod_spec, (batch * nj,), lambda i: (i, 0), tm,
                      w_mix_out, l, alpha)
        xp, conv_p = _ffn_prompt(xp, mod_p, w_ffn, l, batch, seq, alpha)
        outs['shift_p'].append(shift_p.reshape(batch, D_MODEL))
        outs['wkv_p'].append(wkv_p)
        outs['pool_p'].append(pool_p[:, 1:, :])
        outs['conv_p'].append(conv_p)

        *ops, gpool, sgr, shift_s, pool_s = _mix_in_sample(xs, mod_s, shift_all, pool_all, w_mix_in, l)
        yg, wkv_s = _wkv_sample(ops, *w_wkv_cols, wkv_all, outs['wkv_s'], l)
        xs = _mix_out(yg, gpool, sgr, xs, mod_s, sample_mod_spec, (1,), lambda i: (0, 0), nsamp, w_mix_out, l,
                      alpha, yg_transposed=True)
        xs, conv_s = _ffn_sample(xs, mod_s, conv_all, w_ffn, l, alpha)
        outs['shift_s'].append(shift_s)
        outs['wkv_s'].append(wkv_s)
        outs['pool_s'].append(pool_s)
        outs['conv_s'].append(conv_s)
    wkv_s_all = jnp.transpose(outs.pop('wkv_s')[-1], (0, 4, 1, 2, 3))
    st = {k: jnp.stack(v) for k, v in outs.items()}
    return (xp.reshape(batch, seq, D_MODEL), xs.reshape(nsamp, 1, D_MODEL), st['shift_p'], st['wkv_p'], st['pool_p'],
            st['conv_p'], st['shift_s'], wkv_s_all, st['pool_s'], st['conv_s'])


def kernel(x_prompt, x_sample, state_shift, state_wkv, state_pool, state_conv, c_prompt, c_sample, ada_w, ada_b, w_in, mu_shift, decay_w0, decay_up, aaa_a0, aaa_up, gate_up, k_k, k_a, r_k, gn_w, gn_b, w_branch_rwkv, pool_w, pool_scale, w_branch_pool, w_out, ln1_w, ln1_b, ffn_up, conv_w, conv_b, ffn_down, ln2_w, ln2_b):
    p = {'w_in': w_in, 'mu_shift': mu_shift, 'decay_w0': decay_w0, 'decay_up': decay_up, 'aaa_a0': aaa_a0,
         'aaa_up': aaa_up, 'gate_up': gate_up, 'k_k': k_k, 'k_a': k_a, 'r_k': r_k, 'gn_w': gn_w, 'gn_b': gn_b,
         'w_branch_rwkv': w_branch_rwkv, 'pool_w': pool_w, 'pool_scale': pool_scale,
         'w_branch_pool': w_branch_pool, 'w_out': w_out, 'ln1_w': ln1_w, 'ln1_b': ln1_b, 'ffn_up': ffn_up,
         'conv_w': conv_w, 'conv_b': conv_b, 'ffn_down': ffn_down, 'ln2_w': ln2_w, 'ln2_b': ln2_b}
    c_all = jnp.concatenate([c_prompt, c_sample], axis=0).astype(F32)
    mod = _modulation(c_all, ada_w, ada_b)
    dt = x_prompt.dtype
    out = _trunk(x_prompt, x_sample, state_shift, state_wkv, state_pool, state_conv, mod, p)
    return tuple(o.astype(dt) for o in out)
```

```python
import functools
import math

import jax
import jax.numpy as jnp
from jax import lax
from jax.experimental import pallas as pl
from jax.experimental.pallas import tpu as pltpu

F32 = jnp.float32
BF16 = jnp.bfloat16

D_MODEL = 1024
HEAD_SIZE = 64
N_HEADS = D_MODEL // HEAD_SIZE
D_RWKV = N_HEADS * HEAD_SIZE
D_DECAY_LORA = 64
D_AAA_LORA = 64
D_GATE_LORA = 128
D_LORA = D_DECAY_LORA + D_AAA_LORA + D_GATE_LORA
POOL_WINDOWS = (2, 4, 8, 16)
POOL_GROUP = D_MODEL // 8
D_POOL = len(POOL_WINDOWS) * POOL_GROUP
POOL_BUF = max(POOL_WINDOWS) - 1
D_FF = (D_MODEL * 11) // 4
CONV_W = 3
D_SHIFT = 3 * D_RWKV + D_LORA
IN_COLS = D_SHIFT + D_POOL + 2 * D_MODEL
LN_EPS = 1e-5
GN_EPS = 64e-5
EXP_M05 = math.exp(-0.5)

VMEM_LIMIT_BYTES = 56 * 1024 * 1024
ROW_TILE = 256
CHUNK = 64
MIX_OUT_SUBTILES = 4
FFN_SUBTILES = 4
WKV_CHUNKS_PER_STEP = 2
WKV_SEQS_PER_STEP = 2
SAMPLE_VALUE_UNROLL = 4
FF_CHUNK = 1408


def _ln(x):
    mu = jnp.mean(x, axis=-1, keepdims=True)
    xc = x - mu
    var = jnp.mean(xc * xc, axis=-1, keepdims=True)
    return xc * lax.rsqrt(var + LN_EPS)


def _dot(a, b):
    return jnp.dot(a.astype(BF16), b.astype(BF16), preferred_element_type=F32)


def _dot_nt(a, b):
    return lax.dot_general(a.astype(BF16), b.astype(BF16), (((1,), (1,)), ((), ())),
                           preferred_element_type=F32)


def _dot_tn(a, b):
    return lax.dot_general(a.astype(BF16), b.astype(BF16), (((0,), (0,)), ((), ())),
                           preferred_element_type=F32)


def _const_spec(shape):
    nd = len(shape)
    return pl.BlockSpec(shape, lambda *_: (0,) * nd, pipeline_mode=pl.Buffered(1))


def _layer_spec(shape, l):
    nd = len(shape)
    return pl.BlockSpec((None,) + tuple(shape), lambda *_: (l,) + (0,) * nd, pipeline_mode=pl.Buffered(1))


def _params(sem):
    return pltpu.CompilerParams(dimension_semantics=sem, vmem_limit_bytes=VMEM_LIMIT_BYTES)


def _mod_kernel(c_ref, w_ref, b_ref, o_ref):
    o_ref[...] = _dot(c_ref[...], w_ref[...]) + b_ref[...]


def _modulation(c_all, ada_w, ada_b):
    depth, d, n = ada_w.shape
    rows = c_all.shape[0]
    tn = 1536
    return pl.pallas_call(
        _mod_kernel,
        grid=(depth, n // tn),
        in_specs=[pl.BlockSpec((rows, d), lambda l, j: (0, 0)),
                  pl.BlockSpec((None, d, tn), lambda l, j: (l, 0, j)),
                  pl.BlockSpec((None, 1, tn), lambda l, j: (l, 0, j))],
        out_specs=pl.BlockSpec((None, rows, tn), lambda l, j: (l, 0, j)),
        out_shape=jax.ShapeDtypeStruct((depth, rows, n), F32),
        compiler_params=_params(("arbitrary", "arbitrary")),
    )(c_all, ada_w, ada_b.reshape(depth, 1, n))


def _head_sum(x, mpair):
    rows = x.shape[0]
    tiles = range(x.shape[1] // (2 * HEAD_SIZE))
    st = jnp.concatenate([x[:, t * 2 * HEAD_SIZE:(t + 1) * 2 * HEAD_SIZE] for t in tiles], axis=0)
    sm = _dot(st, mpair)
    return jnp.concatenate([sm[t * rows:(t + 1) * rows] for t in tiles], axis=1)


def _rwkv_prep_dots(xs_k, lora, wl_ref, kk_ref, mpair_ref):
    kk = xs_k * kk_ref[...]
    ss = _head_sum(kk * kk, mpair_ref[...])
    lane = lax.broadcasted_iota(jnp.int32, lora.shape, 1)
    act = jnp.where(lane < D_DECAY_LORA, jnp.tanh(lora),
                    jnp.where(lane < D_DECAY_LORA + D_AAA_LORA, lora, jax.nn.sigmoid(lora)))
    z = _dot(act, wl_ref[...])
    return z, kk, ss


def _rwkv_prep_finish(xs_r, xs_k, xs_v, z, kk, ss, w0_ref, a0_ref, ka_ref):
    dec = w0_ref[...] + z[:, 0:D_RWKV]
    lw = -jax.nn.sigmoid(dec) * EXP_M05
    a = jax.nn.sigmoid(a0_ref[...] + z[:, D_RWKV:2 * D_RWKV])
    g = z[:, 2 * D_RWKV:3 * D_RWKV]
    kap = kk / jnp.maximum(jnp.sqrt(ss), 1e-12)
    kmod = xs_k * (1.0 + (a - 1.0) * ka_ref[...])
    return xs_r, kmod, xs_v, kap, kap * a, lw, g


def _pool_mix(pooled, p_in, poolw_ref, pscale_ref, wbp_ref):
    mixed = []
    for gi in range(len(POOL_WINDOWS)):
        sl = slice(gi * POOL_GROUP, (gi + 1) * POOL_GROUP)
        diff = pooled[gi] - p_in[:, sl]
        mixed.append(_dot(diff, poolw_ref[gi]) * pscale_ref[:, sl])
    return _dot(jnp.concatenate(mixed, axis=-1), wbp_ref[...])


def _mix_in_prompt_kernel(x_ref, mod_ref, win_ref, mu_ref, wl_ref, w0_ref, a0_ref, kk_ref, ka_ref,
                          mpair_ref, poolw_ref, pscale_ref, wbp_ref,
                          r_ref, k_ref, v_ref, kap_ref, b_ref, lw_ref, g_ref, gpool_ref, sgr_ref,
                          shift_ref, pool_ref,
                          carry_ref, pext_ref, *, tm):
    j = pl.program_id(1)

    @pl.when(j == 0)
    def _():
        carry_ref[...] = jnp.zeros(carry_ref.shape, F32)
        pext_ref[...] = jnp.zeros(pext_ref.shape, F32)

    x = x_ref[...]
    h = _ln(x) * (1.0 + mod_ref[:, D_MODEL:2 * D_MODEL]) + mod_ref[:, 0:D_MODEL]
    shift_ref[...] = h[tm - 1:tm, :]
    hb = h.astype(BF16)
    row = lax.broadcasted_iota(jnp.int32, (tm, 1), 0)

    def shifted(c0, c1):
        cur = jnp.dot(hb, win_ref[:, c0:c1], preferred_element_type=F32)
        prev = jnp.where(row == 0, carry_ref[7:8, c0:c1], pltpu.roll(cur, 1, 0))
        carry_ref[7:8, c0:c1] = cur[tm - 1:tm, :]
        return cur + (prev - cur) * mu_ref[:, c0:c1]

    xs_k = shifted(D_RWKV, 2 * D_RWKV)
    lora = shifted(3 * D_RWKV, D_SHIFT)
    z, kk, ss = _rwkv_prep_dots(xs_k, lora, wl_ref, kk_ref, mpair_ref)
    p_in = jnp.dot(hb, win_ref[:, D_SHIFT:D_SHIFT + D_POOL], preferred_element_type=F32)
    gates = jnp.dot(hb, win_ref[:, D_SHIFT + D_POOL:IN_COLS], preferred_element_type=F32)

    pext_ref[0:16, :] = pext_ref[tm:tm + 16, :]
    pext_ref[16:16 + tm, :] = p_in
    pos1 = (j * tm + row + 1).astype(F32)
    pooled = []
    for gi, win in enumerate(POOL_WINDOWS):
        sl = slice(gi * POOL_GROUP, (gi + 1) * POOL_GROUP)
        acc = p_in[:, sl]
        for d in range(1, win):
            acc = acc + pext_ref[16 - d:16 - d + tm, sl]
        pooled.append(acc / jnp.minimum(float(win), pos1))
    pool_ref[...] = pext_ref[tm:tm + 16, :]
    y_pool = _pool_mix(pooled, p_in, poolw_ref, pscale_ref, wbp_ref)

    xs_r = shifted(0, D_RWKV)
    xs_v = shifted(2 * D_RWKV, 3 * D_RWKV)
    r, kmod, v, kap, b, lw, g = _rwkv_prep_finish(xs_r, xs_k, xs_v, z, kk, ss, w0_ref, a0_ref, ka_ref)
    r_ref[...] = r
    k_ref[...] = kmod
    v_ref[...] = v
    kap_ref[...] = kap
    b_ref[...] = b
    lw_ref[...] = lw
    g_ref[...] = g.astype(BF16)
    sgr_ref[...] = jax.nn.sigmoid(gates[:, D_MODEL:2 * D_MODEL]).astype(BF16)
    gpool_ref[...] = (jax.nn.sigmoid(gates[:, 0:D_MODEL]) * y_pool).astype(BF16)


def _mix_in_sample_kernel(x_ref, mod_ref, sprev_ref, pst_ref, win_ref, mu_ref, wl_ref, w0_ref, a0_ref, kk_ref,
                          ka_ref, mpair_ref, poolw_ref, pscale_ref, wbp_ref,
                          r_ref, k_ref, v_ref, kap_ref, b_ref, w_ref, g_ref, gpool_ref, sgr_ref,
                          shift_ref, pool_ref):
    x = x_ref[...]
    h = _ln(x) * (1.0 + mod_ref[:, D_MODEL:2 * D_MODEL]) + mod_ref[:, 0:D_MODEL]
    shift_ref[...] = h
    hb = h.astype(BF16)
    pb = sprev_ref[...].astype(BF16)

    def shifted(c0, c1):
        w = win_ref[:, c0:c1]
        cur = jnp.dot(hb, w, preferred_element_type=F32)
        prev = jnp.dot(pb, w, preferred_element_type=F32)
        return cur + (prev - cur) * mu_ref[:, c0:c1]

    xs_r = shifted(0, D_RWKV)
    xs_k = shifted(D_RWKV, 2 * D_RWKV)
    xs_v = shifted(2 * D_RWKV, 3 * D_RWKV)
    lora = shifted(3 * D_RWKV, D_SHIFT)
    z, kk, ss = _rwkv_prep_dots(xs_k, lora, wl_ref, kk_ref, mpair_ref)
    r, kmod, v, kap, b, lw, g = _rwkv_prep_finish(xs_r, xs_k, xs_v, z, kk, ss, w0_ref, a0_ref, ka_ref)
    r_ref[...] = r.T
    k_ref[...] = kmod.T
    v_ref[...] = v.T
    kap_ref[...] = kap.T
    b_ref[...] = b.T
    w_ref[...] = jnp.exp(lw).T
    g_ref[...] = g.T

    p_in = jnp.dot(hb, win_ref[:, D_SHIFT:D_SHIFT + D_POOL], preferred_element_type=F32)
    pooled = []
    for gi, win in enumerate(POOL_WINDOWS):
        sl = slice(gi * POOL_GROUP, (gi + 1) * POOL_GROUP)
        acc = p_in[:, sl]
        for d in range(1, win):
            acc = acc + pst_ref[:, POOL_BUF - d, sl]
        pooled.append(acc / float(win))
    gates = jnp.dot(hb, win_ref[:, D_SHIFT + D_POOL:IN_COLS], preferred_element_type=F32)
    gpool = jax.nn.sigmoid(gates[:, 0:D_MODEL]) * _pool_mix(pooled, p_in, poolw_ref, pscale_ref, wbp_ref)
    gpool_ref[...] = gpool.astype(BF16)
    sgr_ref[...] = jax.nn.sigmoid(gates[:, D_MODEL:2 * D_MODEL]).astype(BF16)
    for i in range(POOL_BUF - 1):
        pool_ref[:, i, :] = pst_ref[:, i + 1, :]
    pool_ref[:, POOL_BUF - 1, :] = p_in


def _mix_in_weight_specs(l):
    return [_layer_spec((D_MODEL, IN_COLS), l),
            _layer_spec((1, D_SHIFT), l),
            _layer_spec((D_LORA, 3 * D_RWKV), l),
            _layer_spec((1, D_RWKV), l),
            _layer_spec((1, D_RWKV), l),
            _layer_spec((1, D_RWKV), l),
            _layer_spec((1, D_RWKV), l),
            _const_spec((2 * HEAD_SIZE, 2 * HEAD_SIZE)),
            _layer_spec((len(POOL_WINDOWS), POOL_GROUP, POOL_GROUP), l),
            _layer_spec((1, D_POOL), l),
            _layer_spec((D_POOL, D_MODEL), l)]


def _mix_in_prompt(x2d, mod3, wts, l, batch, seq):
    tm = min(ROW_TILE, seq)
    nj = seq // tm
    rows = batch * seq
    row_spec = pl.BlockSpec((tm, D_MODEL), lambda b, j: (b * nj + j, 0))
    f32_rows = jax.ShapeDtypeStruct((rows, D_MODEL), F32)
    bf_rows = jax.ShapeDtypeStruct((rows, D_MODEL), BF16)
    return pl.pallas_call(
        functools.partial(_mix_in_prompt_kernel, tm=tm),
        grid=(batch, nj),
        in_specs=[row_spec, pl.BlockSpec((None, 1, 6 * D_MODEL), lambda b, j: (b, 0, 0))] + _mix_in_weight_specs(l),
        out_specs=[row_spec] * 9 + [pl.BlockSpec((None, 1, D_MODEL), lambda b, j: (b, 0, 0)),
                                    pl.BlockSpec((None, 16, D_POOL), lambda b, j: (b, 0, 0))],
        out_shape=[f32_rows] * 6 + [bf_rows] * 3 + [jax.ShapeDtypeStruct((batch, 1, D_MODEL), F32),
                                                    jax.ShapeDtypeStruct((batch, 16, D_POOL), F32)],
        scratch_shapes=[pltpu.VMEM((8, D_SHIFT), F32), pltpu.VMEM((tm + 16, D_POOL), F32)],
        compiler_params=_params(("arbitrary", "arbitrary")),
    )(x2d, mod3, *wts)


def _mix_in_sample(x2d, mod2, shift_all, pool_all, wts, l):
    n = x2d.shape[0]
    out_full = lambda shape: pl.BlockSpec(shape, lambda i: (0,) * len(shape))
    f32_rows = jax.ShapeDtypeStruct((n, D_MODEL), F32)
    f32_cols = jax.ShapeDtypeStruct((D_RWKV, n), F32)
    bf_rows = jax.ShapeDtypeStruct((n, D_MODEL), BF16)
    return pl.pallas_call(
        _mix_in_sample_kernel,
        grid=(1,),
        in_specs=[_const_spec((n, D_MODEL)), _const_spec((n, 6 * D_MODEL)), _layer_spec((n, D_MODEL), l),
                  _layer_spec((n, POOL_BUF, D_POOL), l)] + _mix_in_weight_specs(l),
        out_specs=[out_full((D_RWKV, n))] * 7 + [out_full((n, D_MODEL))] * 3 + [out_full((n, POOL_BUF, D_POOL))],
        out_shape=[f32_cols] * 7 + [bf_rows] * 2 + [f32_rows, jax.ShapeDtypeStruct((n, POOL_BUF, D_POOL), F32)],
        compiler_params=_params(("arbitrary",)),
    )(x2d, mod2, shift_all, pool_all, *wts)


def _wkv_prompt_kernel(r_ref, k_ref, v_ref, kap_ref, b_ref, lw_ref, g_ref, rk_ref, gnw_ref, gnb_ref, mpair_ref,
                       yg_ref, snew_ref, s_ref, *, nch, nseq):
    @pl.when(pl.program_id(1) == 0)
    def _():
        s_ref[...] = jnp.zeros(s_ref.shape, F32)

    L = CHUNK
    pw2 = 2 * HEAD_SIZE
    lane = lax.broadcasted_iota(jnp.int32, (L, pw2), 1)
    trow = lax.broadcasted_iota(jnp.int32, (L, pw2), 0)
    left = lane < HEAD_SIZE
    src = jnp.where(left, lane, lane - HEAD_SIZE)
    strict = src < trow
    incl = src <= trow
    eye = (src == trow).astype(F32)
    bd_mask = ((lax.broadcasted_iota(jnp.int32, (pw2, pw2), 0) < HEAD_SIZE)
               == (lax.broadcasted_iota(jnp.int32, (pw2, pw2), 1) < HEAD_SIZE))

    crow = lax.broadcasted_iota(jnp.int32, (L, 1), 0)

    def bd(x):
        xb = x.astype(BF16)
        zero = jnp.zeros_like(xb)
        return jnp.concatenate([jnp.where(left, xb, zero), jnp.where(left, zero, xb)], axis=0)

    def cumsum_rows(x):
        sh = 1
        while sh < L:
            x = x + jnp.where(crow >= sh, pltpu.roll(x, sh, 0), 0.0)
            sh *= 2
        return x

    seqs = range(nseq)
    chunks = range(nch)
    blocks = [(q, j) for q in seqs for j in chunks]
    pairs = range(N_HEADS // 2)
    cs = [slice(p * pw2, (p + 1) * pw2) for p in pairs]
    units = [(c, p) for c in blocks for p in pairs]

    r, kmod, v, g, dec_last, a_hat, r_hat, b_hat, k_hat, b_til, k_til = ({} for _ in range(11))
    for c in blocks:
        q, j = c
        rs = slice(j * L, (j + 1) * L)
        lw = lw_ref[q, rs, :]
        p_inc = cumsum_rows(lw)
        p_last = p_inc[L - 1:L, :]
        e_p = jnp.exp(p_inc)
        e_np = jnp.exp(-p_inc)
        e_l = jnp.exp(p_last - p_inc)
        dec_last[c] = jnp.exp(p_last)
        r[c], kmod[c], v[c], bb = r_ref[q, rs, :], k_ref[q, rs, :], v_ref[q, rs, :], b_ref[q, rs, :]
        g[c] = g_ref[q, rs, :].astype(F32)
        r_hat[c] = r[c] * e_p
        a_hat[c] = -kap_ref[q, rs, :] * jnp.exp(p_inc - lw)
        b_hat[c] = bb * e_np
        k_hat[c] = kmod[c] * e_np
        b_til[c] = bb * e_l
        k_til[c] = kmod[c] * e_l

    ar = {(c, p): jnp.concatenate([a_hat[c][:, cs[p]], r_hat[c][:, cs[p]]], axis=0).astype(BF16) for c, p in units}
    gram = {(c, p): _dot_nt(ar[c, p], jnp.concatenate([bd(b_hat[c][:, cs[p]]), bd(k_hat[c][:, cs[p]])], axis=0))
            for c, p in units}
    nab = {u_: jnp.where(strict, gram[u_][0:L, 0:pw2], 0.0) for u_ in units}
    nak = {u_: jnp.where(strict, gram[u_][0:L, pw2:2 * pw2], 0.0) for u_ in units}
    rbk = {u_: jnp.concatenate([jnp.where(incl, gram[u_][L:2 * L, 0:pw2], 0.0),
                                jnp.where(incl, gram[u_][L:2 * L, pw2:2 * pw2], 0.0)], axis=1).astype(BF16)
           for u_ in units}
    inv = {u_: eye + nab[u_] for u_ in units}
    pw = {u_: _dot(nab[u_], bd(nab[u_])) for u_ in units}
    for _ in range(4):
        px = {u_: _dot(jnp.concatenate([pw[u_], inv[u_]], axis=0), bd(pw[u_])) for u_ in units}
        pw = {u_: px[u_][0:L] for u_ in units}
        inv = {u_: inv[u_] + px[u_][L:2 * L] for u_ in units}
    inv = {u_: inv[u_] + _dot(inv[u_], bd(pw[u_])) for u_ in units}
    vp = {(c, p): v[c][:, cs[p]] for c, p in units}
    bdv = {u_: bd(vp[u_]) for u_ in units}
    akv = {u_: _dot(nak[u_], bdv[u_]) for u_ in units}

    lanes = [(q, p) for q in seqs for p in pairs]
    s_cur = {qp: s_ref[qp] for qp in lanes}
    y = {}
    for j in chunks:
        ars = {(q, p): _dot_nt(ar[(q, j), p], s_cur[q, p]) for q, p in lanes}
        u = {(q, p): _dot(inv[(q, j), p], bd(ars[q, p][0:L] + akv[(q, j), p])) for q, p in lanes}
        for q, p in lanes:
            y[(q, j), p] = ars[q, p][L:2 * L] + _dot(rbk[(q, j), p],
                                                     jnp.concatenate([bd(u[q, p]), bdv[(q, j), p]], axis=0))
        upd = {(q, p): _dot_tn(jnp.concatenate([u[q, p], vp[(q, j), p]], axis=0),
                               jnp.concatenate([b_til[q, j][:, cs[p]], k_til[q, j][:, cs[p]]], axis=0))
               for q, p in lanes}
        s_cur = {(q, p): s_cur[q, p] * dec_last[q, j][:, cs[p]] + jnp.where(bd_mask, upd[q, p], 0.0)
                 for q, p in lanes}
    for qp in lanes:
        s_ref[qp] = s_cur[qp]
        snew_ref[qp] = s_cur[qp]

    inv_n = 1.0 / HEAD_SIZE
    mpair = mpair_ref[...]
    for c in blocks:
        q, j = c
        y_all = jnp.concatenate([y[c, p] for p in pairs], axis=1)
        mu = _head_sum(y_all, mpair) * inv_n
        yc = y_all - mu
        var = _head_sum(yc * yc, mpair) * inv_n
        yn = yc * lax.rsqrt(var + GN_EPS) * gnw_ref[...] + gnb_ref[...]
        bonus = _head_sum(r[c] * kmod[c] * rk_ref[...], mpair) * v[c]
        yg_ref[q, j * L:(j + 1) * L, :] = ((yn + bonus) * g[c]).astype(BF16)


def _wkv_prompt(ops, rk, gnw, gnb, mpair, l, batch, seq):
    nch = math.gcd(seq // CHUNK, WKV_CHUNKS_PER_STEP)
    nseq = math.gcd(batch, WKV_SEQS_PER_STEP)
    rows_per_step = nch * CHUNK
    nc = seq // rows_per_step
    npair = N_HEADS // 2
    pw2 = 2 * HEAD_SIZE
    view = lambda a: a.reshape(batch // nseq, nseq, seq, D_RWKV)
    row_spec = pl.BlockSpec((None, nseq, rows_per_step, D_RWKV), lambda b, c: (b, 0, c, 0))
    yg, sbd = pl.pallas_call(
        functools.partial(_wkv_prompt_kernel, nch=nch, nseq=nseq),
        grid=(batch // nseq, nc),
        in_specs=[row_spec] * 7 + [_layer_spec((1, D_RWKV), l)] * 3 + [_const_spec((pw2, pw2))],
        out_specs=[row_spec, pl.BlockSpec((None, nseq, npair, pw2, pw2), lambda b, c: (b, 0, 0, 0, 0))],
        out_shape=[jax.ShapeDtypeStruct((batch // nseq, nseq, seq, D_RWKV), BF16),
                   jax.ShapeDtypeStruct((batch // nseq, nseq, npair, pw2, pw2), F32)],
        scratch_shapes=[pltpu.VMEM((nseq, npair, pw2, pw2), F32)],
        compiler_params=_params(("arbitrary", "arbitrary")),
    )(*[view(a) for a in ops], rk, gnw, gnb, mpair)
    yg = yg.reshape(batch * seq, D_RWKV)
    sbd = sbd.reshape(batch, npair, pw2, pw2)
    n = HEAD_SIZE
    wkv = jnp.stack([sbd[:, :, 0:n, 0:n], sbd[:, :, n:2 * n, n:2 * n]], axis=2)
    return yg, wkv.reshape(batch, N_HEADS, n, n)


def _wkv_sample_kernel(r_ref, k_ref, v_ref, kap_ref, b_ref, w_ref, g_ref, rk_ref, gnw_ref, gnb_ref, s_ref, *rest):
    *prev_refs, yg_ref, snew_ref, y_ref = rest
    slot = len(prev_refs)
    for j, prev_ref in enumerate(prev_refs):
        snew_ref[j] = prev_ref[...]
    kap, w, nb, kmod, r = kap_ref[...], w_ref[...], -b_ref[...], k_ref[...], r_ref[...]

    def body(vi, carry):
        row = pl.ds(vi, 1)
        s = s_ref[vi]
        sa = jnp.sum(s * kap, axis=0, keepdims=True)
        s_new = s * w + sa * nb + v_ref[row, :] * kmod
        snew_ref[slot, vi] = s_new
        y_ref[row, :] = jnp.sum(s_new * r, axis=0, keepdims=True)
        return carry

    lax.fori_loop(0, HEAD_SIZE, body, 0, unroll=SAMPLE_VALUE_UNROLL)

    y = y_ref[...]
    mu = jnp.mean(y, axis=0, keepdims=True)
    yc = y - mu
    var = jnp.mean(yc * yc, axis=0, keepdims=True)
    yn = yc * lax.rsqrt(var + GN_EPS) * gnw_ref[...] + gnb_ref[...]
    bonus = jnp.sum(r * kmod * rk_ref[...], axis=0, keepdims=True) * v_ref[...]
    yg_ref[...] = (yn + bonus) * g_ref[...]


def _wkv_sample(ops, rk, gnw, gnb, state_all, prev_new, l):
    depth, n = state_all.shape[0], state_all.shape[-1]
    prev = list(prev_new) if l == depth - 1 else []
    slots = len(prev) + 1
    hn = HEAD_SIZE
    col_spec = pl.BlockSpec((hn, n), lambda h: (h, 0))
    par_spec = pl.BlockSpec((None, hn, 1), lambda h: (l, h, 0))
    tile = (hn, hn, n)
    st_in = pl.BlockSpec((None, None) + tile, lambda h: (l, h, 0, 0, 0))
    prev_spec = pl.BlockSpec((None, None) + tile, lambda h: (0, h, 0, 0, 0))
    st_out = pl.BlockSpec((slots, None) + tile, lambda h: (0, h, 0, 0, 0))
    return pl.pallas_call(
        _wkv_sample_kernel,
        grid=(N_HEADS,),
        in_specs=[col_spec] * 7 + [par_spec] * 3 + [st_in] + [prev_spec] * len(prev),
        out_specs=[col_spec, st_out],
        out_shape=[jax.ShapeDtypeStruct((D_RWKV, n), F32),
                   jax.ShapeDtypeStruct((slots, N_HEADS) + tile, F32)],
        scratch_shapes=[pltpu.VMEM((hn, n), F32)],
        compiler_params=_params(("arbitrary",)),
    )(*ops, rk, gnw, gnb, state_all, *prev)


def _mix_out_kernel(yg_ref, gpool_ref, sgr_ref, x_ref, mod_ref, wbr_ref, wout_ref, lnw_ref, lnb_ref, o_ref, *, alpha,
                    yg_transposed, nsub):
    sub = o_ref.shape[0] // nsub
    rows = [slice(i * sub, (i + 1) * sub) for i in range(nsub)]
    if yg_transposed:
        yg = [yg_ref[...].T]
    else:
        yg = [yg_ref[rs, :] for rs in rows]
    y_rwkv = [_dot(yg[i], wbr_ref[...]) for i in range(nsub)]
    merged = [gpool_ref[rs, :].astype(F32) + sgr_ref[rs, :].astype(F32) * y_rwkv[i] for i, rs in enumerate(rows)]
    mix = [_dot(merged[i], wout_ref[...]) for i in range(nsub)]
    g1 = mod_ref[:, 2 * D_MODEL:3 * D_MODEL]
    for i, rs in enumerate(rows):
        o_ref[rs, :] = _ln(alpha * x_ref[rs, :] + (1.0 + g1) * mix[i]) * lnw_ref[...] + lnb_ref[...]


def _mix_out(yg, gpool, sgr, x2d, mod, mod_spec, grid, row_map, tm, wts, l, alpha, yg_transposed=False, nsub=1):
    row_spec = pl.BlockSpec((tm, D_MODEL), row_map)
    yg_spec = _const_spec(yg.shape) if yg_transposed else row_spec
    return pl.pallas_call(
        functools.partial(_mix_out_kernel, alpha=alpha, yg_transposed=yg_transposed, nsub=nsub),
        grid=grid,
        in_specs=[yg_spec] + [row_spec] * 3
        + [mod_spec, _layer_spec((D_RWKV, D_MODEL), l), _layer_spec((D_MODEL, D_MODEL), l),
           _layer_spec((1, D_MODEL), l), _layer_spec((1, D_MODEL), l)],
        out_specs=row_spec,
        out_shape=jax.ShapeDtypeStruct(x2d.shape, F32),
        compiler_params=_params(("arbitrary",) * len(grid)),
    )(yg, gpool, sgr, x2d, mod, *wts)


def _ffn_prompt_kernel(x_ref, mod_ref, up_ref, cw_ref, cb_ref, down_ref, lnw_ref, lnb_ref,
                       o_ref, conv_ref, carry_ref, *, sub, nsub, alpha):
    @pl.when(pl.program_id(1) == 0)
    def _():
        carry_ref[...] = jnp.zeros(carry_ref.shape, F32)

    rows = [slice(i * sub, (i + 1) * sub) for i in range(nsub)]
    row = lax.broadcasted_iota(jnp.int32, (sub, 1), 0)
    hb = []
    for rs in rows:
        h2 = _ln(x_ref[rs, :]) * (1.0 + mod_ref[:, 4 * D_MODEL:5 * D_MODEL]) + mod_ref[:, 3 * D_MODEL:4 * D_MODEL]
        hb.append(h2.astype(BF16))

    def conv(i, c0, c1):
        u = jnp.dot(hb[i], up_ref[:, c0:c1], preferred_element_type=F32)
        c6 = carry_ref[6:7, c0:c1]
        c7 = carry_ref[7:8, c0:c1]
        p1 = jnp.where(row == 0, c7, pltpu.roll(u, 1, 0))
        p2 = jnp.where(row == 0, c6, jnp.where(row == 1, c7, pltpu.roll(u, 2, 0)))
        carry_ref[6:8, c0:c1] = u[sub - 2:sub, :]
        if i == nsub - 1:
            conv_ref[:, c0:c1] = u[sub - 2:sub, :]
        return cb_ref[:, c0:c1] + p2 * cw_ref[0:1, c0:c1] + p1 * cw_ref[1:2, c0:c1] + u * cw_ref[2:3, c0:c1]

    nchunk = D_FF // FF_CHUNK
    act = []
    for i in range(nsub):
        val = [conv(i, c * FF_CHUNK, (c + 1) * FF_CHUNK) for c in range(nchunk)]
        gate = [conv(i, D_FF + c * FF_CHUNK, D_FF + (c + 1) * FF_CHUNK) for c in range(nchunk)]
        act.append([(val[c] * jax.nn.silu(gate[c])).astype(BF16) for c in range(nchunk)])
    ff = []
    for i in range(nsub):
        acc = jnp.zeros((sub, D_MODEL), F32)
        for c in range(nchunk):
            acc = acc + jnp.dot(act[i][c], down_ref[c * FF_CHUNK:(c + 1) * FF_CHUNK, :], preferred_element_type=F32)
        ff.append(acc)
    g2 = mod_ref[:, 5 * D_MODEL:6 * D_MODEL]
    for i, rs in enumerate(rows):
        o_ref[rs, :] = _ln(alpha * x_ref[rs, :] + (1.0 + g2) * ff[i]) * lnw_ref[...] + lnb_ref[...]


def _ffn_weight_specs(l):
    return [_layer_spec((D_MODEL, 2 * D_FF), l), _layer_spec((CONV_W, 2 * D_FF), l), _layer_spec((1, 2 * D_FF), l),
            _layer_spec((D_FF, D_MODEL), l), _layer_spec((1, D_MODEL), l), _layer_spec((1, D_MODEL), l)]


def _ffn_prompt(x2d, mod3, wts, l, batch, seq, alpha):
    sub = min(ROW_TILE, seq)
    nsub = math.gcd(seq // sub, FFN_SUBTILES)
    tm = sub * nsub
    nj = seq // tm
    row_spec = pl.BlockSpec((tm, D_MODEL), lambda b, j: (b * nj + j, 0))
    return pl.pallas_call(
        functools.partial(_ffn_prompt_kernel, sub=sub, nsub=nsub, alpha=alpha),
        grid=(batch, nj),
        in_specs=[row_spec, pl.BlockSpec((None, 1, 6 * D_MODEL), lambda b, j: (b, 0, 0))] + _ffn_weight_specs(l),
        out_specs=[row_spec, pl.BlockSpec((None, CONV_W - 1, 2 * D_FF), lambda b, j: (b, 0, 0))],
        out_shape=[jax.ShapeDtypeStruct(x2d.shape, F32),
                   jax.ShapeDtypeStruct((batch, CONV_W - 1, 2 * D_FF), F32)],
        scratch_shapes=[pltpu.VMEM((8, 2 * D_FF), F32)],
        compiler_params=_params(("arbitrary", "arbitrary")),
    )(x2d, mod3, *wts)


def _ffn_sample_kernel(x_ref, mod_ref, cst_ref, up_ref, cw_ref, cb_ref, down_ref, lnw_ref, lnb_ref,
                       o_ref, conv_ref, *, alpha):
    x = x_ref[...]
    h2 = _ln(x) * (1.0 + mod_ref[:, 4 * D_MODEL:5 * D_MODEL]) + mod_ref[:, 3 * D_MODEL:4 * D_MODEL]
    hb = h2.astype(BF16)

    def conv(c0, c1):
        u = jnp.dot(hb, up_ref[:, c0:c1], preferred_element_type=F32)
        p2 = cst_ref[:, 0, c0:c1]
        p1 = cst_ref[:, 1, c0:c1]
        conv_ref[:, 0, c0:c1] = p1
        conv_ref[:, 1, c0:c1] = u
        return cb_ref[:, c0:c1] + p2 * cw_ref[0:1, c0:c1] + p1 * cw_ref[1:2, c0:c1] + u * cw_ref[2:3, c0:c1]

    ff = jnp.zeros(x.shape, F32)
    for c in range(D_FF // FF_CHUNK):
        val = conv(c * FF_CHUNK, (c + 1) * FF_CHUNK)
        gate = conv(D_FF + c * FF_CHUNK, D_FF + (c + 1) * FF_CHUNK)
        ff = ff + _dot(val * jax.nn.silu(gate), down_ref[c * FF_CHUNK:(c + 1) * FF_CHUNK, :])
    g2 = mod_ref[:, 5 * D_MODEL:6 * D_MODEL]
    o_ref[...] = _ln(alpha * x + (1.0 + g2) * ff) * lnw_ref[...] + lnb_ref[...]


def _ffn_sample(x2d, mod2, conv_all, wts, l, alpha):
    n = x2d.shape[0]
    st = (n, CONV_W - 1, 2 * D_FF)
    out_full = lambda shape: pl.BlockSpec(shape, lambda i: (0,) * len(shape))
    return pl.pallas_call(
        functools.partial(_ffn_sample_kernel, alpha=alpha),
        grid=(1,),
        in_specs=[_const_spec((n, D_MODEL)), _const_spec((n, 6 * D_MODEL)), _layer_spec(st, l)]
        + _ffn_weight_specs(l),
        out_specs=[out_full((n, D_MODEL)), out_full(st)],
        out_shape=[jax.ShapeDtypeStruct(x2d.shape, F32), jax.ShapeDtypeStruct(st, F32)],
        compiler_params=_params(("arbitrary",)),
    )(x2d, mod2, conv_all, *wts)


def _stacked_weights(p):
    depth = p['w_in'].shape[0]
    row = lambda a: a.reshape(depth, 1, -1)
    bf = lambda a: a.astype(BF16)
    zeros = lambda r, c: jnp.zeros((depth, r, c), F32)
    lora_up = jnp.concatenate([
        jnp.concatenate([p['decay_up'], zeros(D_DECAY_LORA, 2 * D_RWKV)], axis=2),
        jnp.concatenate([zeros(D_AAA_LORA, D_RWKV), p['aaa_up'], zeros(D_AAA_LORA, D_RWKV)], axis=2),
        jnp.concatenate([zeros(D_GATE_LORA, 2 * D_RWKV), p['gate_up']], axis=2)], axis=1)
    head = jnp.arange(2 * HEAD_SIZE) // HEAD_SIZE
    mpair = (head[:, None] == head[None, :]).astype(BF16)
    mix_in = (bf(p['w_in']), row(p['mu_shift']), bf(lora_up), row(p['decay_w0']), row(p['aaa_a0']),
              row(p['k_k']), row(p['k_a']), mpair, bf(p['pool_w']), row(p['pool_scale']), bf(p['w_branch_pool']))
    wkv = (row(p['r_k']), row(p['gn_w']), row(p['gn_b']))
    col = lambda a: a.reshape(depth, -1, 1)
    wkv_cols = (col(p['r_k']), col(p['gn_w']), col(p['gn_b']))
    mix_out = (bf(p['w_branch_rwkv']), bf(p['w_out']), row(p['ln1_w']), row(p['ln1_b']))
    ffn = (bf(p['ffn_up']), p['conv_w'], row(p['conv_b']), bf(p['ffn_down']), row(p['ln2_w']), row(p['ln2_b']))
    return mix_in, wkv, wkv_cols, mix_out, ffn


def _trunk(x_prompt, x_sample, state_shift, state_wkv, state_pool, state_conv, mod, p):
    depth = p['w_in'].shape[0]
    alpha = float((2 * depth) ** 0.25)
    batch, seq, _ = x_prompt.shape
    nsamp = x_sample.shape[0]
    xp = x_prompt.reshape(batch * seq, D_MODEL).astype(F32)
    xs = x_sample.reshape(nsamp, D_MODEL).astype(F32)
    tm = min(ROW_TILE, seq)
    nj = seq // tm
    mo_sub = math.gcd(nj, MIX_OUT_SUBTILES)
    prompt_mod_spec = pl.BlockSpec((None, 1, 6 * D_MODEL), lambda i: (i // (nj // mo_sub), 0, 0))
    sample_mod_spec = pl.BlockSpec((nsamp, 6 * D_MODEL), lambda i: (0, 0))
    w_mix_in, w_wkv, w_wkv_cols, w_mix_out, w_ffn = _stacked_weights(p)
    shift_all = state_shift.astype(F32)
    wkv_all = jnp.transpose(state_wkv.astype(F32), (0, 2, 3, 4, 1))
    pool_all = state_pool.astype(F32)
    conv_all = state_conv.astype(F32)
    outs = {k: [] for k in ('shift_p', 'wkv_p', 'pool_p', 'conv_p', 'shift_s', 'wkv_s', 'pool_s', 'conv_s')}
    for l in range(depth):
        mod_p = mod[l, 0:batch].reshape(batch, 1, 6 * D_MODEL)
        mod_s = mod[l, batch:batch + nsamp]

        *ops, gpool, sgr, shift_p, pool_p = _mix_in_prompt(xp, mod_p, w_mix_in, l, batch, seq)
        yg, wkv_p = _wkv_prompt(ops, *w_wkv, w_mix_in[7], l, batch, seq)
        xp = _mix_out(yg, gpool, sgr, xp, mod_p, prompt_mod_spec, (batch * nj // mo_sub,), lambda i: (i, 0),
                      tm * mo_sub, w_mix_out, l, alpha, nsub=mo_sub)
        xp, conv_p = _ffn_prompt(xp, mod_p, w_ffn, l, batch, seq, alpha)
        outs['shift_p'].append(shift_p.reshape(batch, D_MODEL))
        outs['wkv_p'].append(wkv_p)
        outs['pool_p'].append(pool_p[:, 1:, :])
        outs['conv_p'].append(conv_p)

        *ops, gpool, sgr, shift_s, pool_s = _mix_in_sample(xs, mod_s, shift_all, pool_all, w_mix_in, l)
        yg, wkv_s = _wkv_sample(ops, *w_wkv_cols, wkv_all, outs['wkv_s'], l)
        xs = _mix_out(yg, gpool, sgr, xs, mod_s, sample_mod_spec, (1,), lambda i: (0, 0), nsamp, w_mix_out, l,
                      alpha, yg_transposed=True)
        xs, conv_s = _ffn_sample(xs, mod_s, conv_all, w_ffn, l, alpha)
        outs['shift_s'].append(shift_s)
        outs['wkv_s'].append(wkv_s)
        outs['pool_s'].append(pool_s)
        outs['conv_s'].append(conv_s)
    wkv_s_all = jnp.transpose(outs.pop('wkv_s')[-1], (0, 4, 1, 2, 3))
    st = {k: jnp.stack(v) for k, v in outs.items()}
    return (xp.reshape(batch, seq, D_MODEL), xs.reshape(nsamp, 1, D_MODEL), st['shift_p'], st['wkv_p'], st['pool_p'],
            st['conv_p'], st['shift_s'], wkv_s_all, st['pool_s'], st['conv_s'])


def kernel(x_prompt, x_sample, state_shift, state_wkv, state_pool, state_conv, c_prompt, c_sample, ada_w, ada_b, w_in, mu_shift, decay_w0, decay_up, aaa_a0, aaa_up, gate_up, k_k, k_a, r_k, gn_w, gn_b, w_branch_rwkv, pool_w, pool_scale, w_branch_pool, w_out, ln1_w, ln1_b, ffn_up, conv_w, conv_b, ffn_down, ln2_w, ln2_b):
    p = {'w_in': w_in, 'mu_shift': mu_shift, 'decay_w0': decay_w0, 'decay_up': decay_up, 'aaa_a0': aaa_a0,
         'aaa_up': aaa_up, 'gate_up': gate_up, 'k_k': k_k, 'k_a': k_a, 'r_k': r_k, 'gn_w': gn_w, 'gn_b': gn_b,
         'w_branch_rwkv': w_branch_rwkv, 'pool_w': pool_w, 'pool_scale': pool_scale,
         'w_branch_pool': w_branch_pool, 'w_out': w_out, 'ln1_w': ln1_w, 'ln1_b': ln1_b, 'ffn_up': ffn_up,
         'conv_w': conv_w, 'conv_b': conv_b, 'ffn_down': ffn_down, 'ln2_w': ln2_w, 'ln2_b': ln2_b}
    c_all = jnp.concatenate([c_prompt, c_sample], axis=0).astype(F32)
    mod = _modulation(c_all, ada_w, ada_b)
    dt = x_prompt.dtype
    out = _trunk(x_prompt, x_sample, state_shift, state_wkv, state_pool, state_conv, mod, p)
    return tuple(o.astype(dt) for o in out)
```

```python
import functools
import math

import jax
import jax.numpy as jnp
from jax import lax
from jax.experimental import pallas as pl
from jax.experimental.pallas import tpu as pltpu

F32 = jnp.float32
BF16 = jnp.bfloat16

D_MODEL = 1024
HEAD_SIZE = 64
N_HEADS = D_MODEL // HEAD_SIZE
D_RWKV = N_HEADS * HEAD_SIZE
D_DECAY_LORA = 64
D_AAA_LORA = 64
D_GATE_LORA = 128
D_LORA = D_DECAY_LORA + D_AAA_LORA + D_GATE_LORA
POOL_WINDOWS = (2, 4, 8, 16)
POOL_GROUP = D_MODEL // 8
D_POOL = len(POOL_WINDOWS) * POOL_GROUP
POOL_BUF = max(POOL_WINDOWS) - 1
D_FF = (D_MODEL * 11) // 4
CONV_W = 3
D_SHIFT = 3 * D_RWKV + D_LORA
IN_COLS = D_SHIFT + D_POOL + 2 * D_MODEL
LN_EPS = 1e-5
GN_EPS = 64e-5
EXP_M05 = math.exp(-0.5)

VMEM_LIMIT_BYTES = 56 * 1024 * 1024
ROW_TILE = 256
CHUNK = 64
MIX_IN_SUBTILES = 2
MIX_OUT_SUBTILES = 4
FFN_SUBTILES = 4
WKV_CHUNKS_PER_STEP = 2
WKV_SEQS_PER_STEP = 2
SAMPLE_VALUE_UNROLL = 4
FF_CHUNK = 2816


def _ln(x):
    mu = jnp.mean(x, axis=-1, keepdims=True)
    xc = x - mu
    var = jnp.mean(xc * xc, axis=-1, keepdims=True)
    return xc * lax.rsqrt(var + LN_EPS)


def _dot(a, b):
    return jnp.dot(a.astype(BF16), b.astype(BF16), preferred_element_type=F32)


def _dot_nt(a, b):
    return lax.dot_general(a.astype(BF16), b.astype(BF16), (((1,), (1,)), ((), ())),
                           preferred_element_type=F32)


def _dot_tn(a, b):
    return lax.dot_general(a.astype(BF16), b.astype(BF16), (((0,), (0,)), ((), ())),
                           preferred_element_type=F32)


def _const_spec(shape):
    nd = len(shape)
    return pl.BlockSpec(shape, lambda *_: (0,) * nd, pipeline_mode=pl.Buffered(1))


def _layer_spec(shape, l):
    nd = len(shape)
    return pl.BlockSpec((None,) + tuple(shape), lambda *_: (l,) + (0,) * nd, pipeline_mode=pl.Buffered(1))


def _params(sem):
    return pltpu.CompilerParams(dimension_semantics=sem, vmem_limit_bytes=VMEM_LIMIT_BYTES)


def _mod_kernel(c_ref, w_ref, b_ref, o_ref):
    o_ref[...] = _dot(c_ref[...], w_ref[...]) + b_ref[...]


def _modulation(c_all, ada_w, ada_b):
    depth, d, n = ada_w.shape
    rows = c_all.shape[0]
    tn = 1536
    return pl.pallas_call(
        _mod_kernel,
        grid=(depth, n // tn),
        in_specs=[pl.BlockSpec((rows, d), lambda l, j: (0, 0)),
                  pl.BlockSpec((None, d, tn), lambda l, j: (l, 0, j)),
                  pl.BlockSpec((None, 1, tn), lambda l, j: (l, 0, j))],
        out_specs=pl.BlockSpec((None, rows, tn), lambda l, j: (l, 0, j)),
        out_shape=jax.ShapeDtypeStruct((depth, rows, n), F32),
        compiler_params=_params(("arbitrary", "arbitrary")),
    )(c_all, ada_w, ada_b.reshape(depth, 1, n))


def _head_sum(x, mpair):
    rows = x.shape[0]
    tiles = range(x.shape[1] // (2 * HEAD_SIZE))
    st = jnp.concatenate([x[:, t * 2 * HEAD_SIZE:(t + 1) * 2 * HEAD_SIZE] for t in tiles], axis=0)
    sm = _dot(st, mpair)
    return jnp.concatenate([sm[t * rows:(t + 1) * rows] for t in tiles], axis=1)


def _rwkv_prep_dots(xs_k, lora, wl_ref, kk_ref, mpair_ref):
    kk = xs_k * kk_ref[...]
    ss = _head_sum(kk * kk, mpair_ref[...])
    lane = lax.broadcasted_iota(jnp.int32, lora.shape, 1)
    act = jnp.where(lane < D_DECAY_LORA, jnp.tanh(lora),
                    jnp.where(lane < D_DECAY_LORA + D_AAA_LORA, lora, jax.nn.sigmoid(lora)))
    z = _dot(act, wl_ref[...])
    return z, kk, ss


def _rwkv_prep_finish(xs_r, xs_k, xs_v, z, kk, ss, w0_ref, a0_ref, ka_ref):
    dec = w0_ref[...] + z[:, 0:D_RWKV]
    lw = -jax.nn.sigmoid(dec) * EXP_M05
    a = jax.nn.sigmoid(a0_ref[...] + z[:, D_RWKV:2 * D_RWKV])
    g = z[:, 2 * D_RWKV:3 * D_RWKV]
    kap = kk / jnp.maximum(jnp.sqrt(ss), 1e-12)
    kmod = xs_k * (1.0 + (a - 1.0) * ka_ref[...])
    return xs_r, kmod, xs_v, kap, kap * a, lw, g


def _pool_mix(pooled, p_in, poolw_ref, pscale_ref, wbp_ref):
    mixed = []
    for gi in range(len(POOL_WINDOWS)):
        sl = slice(gi * POOL_GROUP, (gi + 1) * POOL_GROUP)
        diff = pooled[gi] - p_in[:, sl]
        mixed.append(_dot(diff, poolw_ref[gi]) * pscale_ref[:, sl])
    return _dot(jnp.concatenate(mixed, axis=-1), wbp_ref[...])


def _mix_in_prompt_kernel(x_ref, mod_ref, win_ref, mu_ref, wl_ref, w0_ref, a0_ref, kk_ref, ka_ref,
                          mpair_ref, poolw_ref, pscale_ref, wbp_ref,
                          r_ref, k_ref, v_ref, kap_ref, b_ref, lw_ref, g_ref, gpool_ref, sgr_ref,
                          shift_ref, pool_ref,
                          carry_ref, pext_ref, *, sub, nsub):
    j = pl.program_id(1)
    tm = sub * nsub

    @pl.when(j == 0)
    def _():
        carry_ref[...] = jnp.zeros(carry_ref.shape, F32)
        pext_ref[...] = jnp.zeros(pext_ref.shape, F32)

    tiles = range(nsub)
    rows = [slice(i * sub, (i + 1) * sub) for i in tiles]
    row = lax.broadcasted_iota(jnp.int32, (sub, 1), 0)
    hb = []
    for i in tiles:
        h = _ln(x_ref[rows[i], :]) * (1.0 + mod_ref[:, D_MODEL:2 * D_MODEL]) + mod_ref[:, 0:D_MODEL]
        if i == nsub - 1:
            shift_ref[...] = h[sub - 1:sub, :]
        hb.append(h.astype(BF16))

    def shifted(i, c0, c1):
        cur = jnp.dot(hb[i], win_ref[:, c0:c1], preferred_element_type=F32)
        prev = jnp.where(row == 0, carry_ref[7:8, c0:c1], pltpu.roll(cur, 1, 0))
        carry_ref[7:8, c0:c1] = cur[sub - 1:sub, :]
        return cur + (prev - cur) * mu_ref[:, c0:c1]

    xs_k = [shifted(i, D_RWKV, 2 * D_RWKV) for i in tiles]
    lora = [shifted(i, 3 * D_RWKV, D_SHIFT) for i in tiles]
    prep = [_rwkv_prep_dots(xs_k[i], lora[i], wl_ref, kk_ref, mpair_ref) for i in tiles]
    p_in = [jnp.dot(hb[i], win_ref[:, D_SHIFT:D_SHIFT + D_POOL], preferred_element_type=F32) for i in tiles]
    gates = [jnp.dot(hb[i], win_ref[:, D_SHIFT + D_POOL:IN_COLS], preferred_element_type=F32) for i in tiles]

    pext_ref[0:16, :] = pext_ref[tm:tm + 16, :]
    for i in tiles:
        pext_ref[16 + i * sub:16 + (i + 1) * sub, :] = p_in[i]
    pool_ref[...] = pext_ref[tm:tm + 16, :]
    y_pool = []
    for i in tiles:
        base = 16 + i * sub
        pos1 = (j * tm + i * sub + row + 1).astype(F32)
        pooled = []
        for gi, win in enumerate(POOL_WINDOWS):
            sl = slice(gi * POOL_GROUP, (gi + 1) * POOL_GROUP)
            acc = p_in[i][:, sl]
            for d in range(1, win):
                acc = acc + pext_ref[base - d:base - d + sub, sl]
            pooled.append(acc / jnp.minimum(float(win), pos1))
        y_pool.append(_pool_mix(pooled, p_in[i], poolw_ref, pscale_ref, wbp_ref))

    xs_r = [shifted(i, 0, D_RWKV) for i in tiles]
    xs_v = [shifted(i, 2 * D_RWKV, 3 * D_RWKV) for i in tiles]
    for i in tiles:
        rs = rows[i]
        r, kmod, v, kap, b, lw, g = _rwkv_prep_finish(xs_r[i], xs_k[i], xs_v[i], *prep[i], w0_ref, a0_ref, ka_ref)
        r_ref[rs, :] = r.astype(BF16)
        k_ref[rs, :] = kmod.astype(BF16)
        v_ref[rs, :] = v.astype(BF16)
        kap_ref[rs, :] = kap.astype(BF16)
        b_ref[rs, :] = b.astype(BF16)
        lw_ref[rs, :] = lw
        g_ref[rs, :] = g.astype(BF16)
        sgr_ref[rs, :] = jax.nn.sigmoid(gates[i][:, D_MODEL:2 * D_MODEL]).astype(BF16)
        gpool_ref[rs, :] = (jax.nn.sigmoid(gates[i][:, 0:D_MODEL]) * y_pool[i]).astype(BF16)


def _mix_in_sample_kernel(x_ref, mod_ref, sprev_ref, pst_ref, win_ref, mu_ref, wl_ref, w0_ref, a0_ref, kk_ref,
                          ka_ref, mpair_ref, poolw_ref, pscale_ref, wbp_ref,
                          r_ref, k_ref, v_ref, kap_ref, b_ref, w_ref, g_ref, gpool_ref, sgr_ref,
                          shift_ref, pool_ref):
    x = x_ref[...]
    h = _ln(x) * (1.0 + mod_ref[:, D_MODEL:2 * D_MODEL]) + mod_ref[:, 0:D_MODEL]
    shift_ref[...] = h
    hb = h.astype(BF16)
    pb = sprev_ref[...].astype(BF16)

    def shifted(c0, c1):
        w = win_ref[:, c0:c1]
        cur = jnp.dot(hb, w, preferred_element_type=F32)
        prev = jnp.dot(pb, w, preferred_element_type=F32)
        return cur + (prev - cur) * mu_ref[:, c0:c1]

    xs_r = shifted(0, D_RWKV)
    xs_k = shifted(D_RWKV, 2 * D_RWKV)
    xs_v = shifted(2 * D_RWKV, 3 * D_RWKV)
    lora = shifted(3 * D_RWKV, D_SHIFT)
    z, kk, ss = _rwkv_prep_dots(xs_k, lora, wl_ref, kk_ref, mpair_ref)
    r, kmod, v, kap, b, lw, g = _rwkv_prep_finish(xs_r, xs_k, xs_v, z, kk, ss, w0_ref, a0_ref, ka_ref)
    r_ref[...] = r.T
    k_ref[...] = kmod.T
    v_ref[...] = v.T
    kap_ref[...] = kap.T
    b_ref[...] = b.T
    w_ref[...] = jnp.exp(lw).T
    g_ref[...] = g.T

    p_in = jnp.dot(hb, win_ref[:, D_SHIFT:D_SHIFT + D_POOL], preferred_element_type=F32)
    pooled = []
    for gi, win in enumerate(POOL_WINDOWS):
        sl = slice(gi * POOL_GROUP, (gi + 1) * POOL_GROUP)
        acc = p_in[:, sl]
        for d in range(1, win):
            acc = acc + pst_ref[:, POOL_BUF - d, sl]
        pooled.append(acc / float(win))
    gates = jnp.dot(hb, win_ref[:, D_SHIFT + D_POOL:IN_COLS], preferred_element_type=F32)
    gpool = jax.nn.sigmoid(gates[:, 0:D_MODEL]) * _pool_mix(pooled, p_in, poolw_ref, pscale_ref, wbp_ref)
    gpool_ref[...] = gpool.astype(BF16)
    sgr_ref[...] = jax.nn.sigmoid(gates[:, D_MODEL:2 * D_MODEL]).astype(BF16)
    for i in range(POOL_BUF - 1):
        pool_ref[:, i, :] = pst_ref[:, i + 1, :]
    pool_ref[:, POOL_BUF - 1, :] = p_in


def _mix_in_weight_specs(l):
    return [_layer_spec((D_MODEL, IN_COLS), l),
            _layer_spec((1, D_SHIFT), l),
            _layer_spec((D_LORA, 3 * D_RWKV), l),
            _layer_spec((1, D_RWKV), l),
            _layer_spec((1, D_RWKV), l),
            _layer_spec((1, D_RWKV), l),
            _layer_spec((1, D_RWKV), l),
            _const_spec((2 * HEAD_SIZE, 2 * HEAD_SIZE)),
            _layer_spec((len(POOL_WINDOWS), POOL_GROUP, POOL_GROUP), l),
            _layer_spec((1, D_POOL), l),
            _layer_spec((D_POOL, D_MODEL), l)]


def _mix_in_prompt(x2d, mod3, wts, l, batch, seq):
    sub = min(ROW_TILE, seq)
    nsub = math.gcd(seq // sub, MIX_IN_SUBTILES)
    tm = sub * nsub
    nj = seq // tm
    rows = batch * seq
    row_spec = pl.BlockSpec((tm, D_MODEL), lambda b, j: (b * nj + j, 0))
    f32_rows = jax.ShapeDtypeStruct((rows, D_MODEL), F32)
    bf_rows = jax.ShapeDtypeStruct((rows, D_MODEL), BF16)
    return pl.pallas_call(
        functools.partial(_mix_in_prompt_kernel, sub=sub, nsub=nsub),
        grid=(batch, nj),
        in_specs=[row_spec, pl.BlockSpec((None, 1, 6 * D_MODEL), lambda b, j: (b, 0, 0))] + _mix_in_weight_specs(l),
        out_specs=[row_spec] * 9 + [pl.BlockSpec((None, 1, D_MODEL), lambda b, j: (b, 0, 0)),
                                    pl.BlockSpec((None, 16, D_POOL), lambda b, j: (b, 0, 0))],
        out_shape=[bf_rows] * 5 + [f32_rows] + [bf_rows] * 3 + [jax.ShapeDtypeStruct((batch, 1, D_MODEL), F32),
                                                                jax.ShapeDtypeStruct((batch, 16, D_POOL), F32)],
        scratch_shapes=[pltpu.VMEM((8, D_SHIFT), F32), pltpu.VMEM((tm + 16, D_POOL), F32)],
        compiler_params=_params(("arbitrary", "arbitrary")),
    )(x2d, mod3, *wts)


def _mix_in_sample(x2d, mod2, shift_all, pool_all, wts, l):
    n = x2d.shape[0]
    out_full = lambda shape: pl.BlockSpec(shape, lambda i: (0,) * len(shape))
    f32_rows = jax.ShapeDtypeStruct((n, D_MODEL), F32)
    f32_cols = jax.ShapeDtypeStruct((D_RWKV, n), F32)
    bf_rows = jax.ShapeDtypeStruct((n, D_MODEL), BF16)
    return pl.pallas_call(
        _mix_in_sample_kernel,
        grid=(1,),
        in_specs=[_const_spec((n, D_MODEL)), _const_spec((n, 6 * D_MODEL)), _layer_spec((n, D_MODEL), l),
                  _layer_spec((n, POOL_BUF, D_POOL), l)] + _mix_in_weight_specs(l),
        out_specs=[out_full((D_RWKV, n))] * 7 + [out_full((n, D_MODEL))] * 3 + [out_full((n, POOL_BUF, D_POOL))],
        out_shape=[f32_cols] * 7 + [bf_rows] * 2 + [f32_rows, jax.ShapeDtypeStruct((n, POOL_BUF, D_POOL), F32)],
        compiler_params=_params(("arbitrary",)),
    )(x2d, mod2, shift_all, pool_all, *wts)


def _wkv_prompt_kernel(r_ref, k_ref, v_ref, kap_ref, b_ref, lw_ref, g_ref, rk_ref, gnw_ref, gnb_ref, mpair_ref,
                       yg_ref, snew_ref, s_ref, *, nch, nseq):
    @pl.when(pl.program_id(1) == 0)
    def _():
        s_ref[...] = jnp.zeros(s_ref.shape, F32)

    L = CHUNK
    pw2 = 2 * HEAD_SIZE
    lane = lax.broadcasted_iota(jnp.int32, (L, pw2), 1)
    trow = lax.broadcasted_iota(jnp.int32, (L, pw2), 0)
    left = lane < HEAD_SIZE
    src = jnp.where(left, lane, lane - HEAD_SIZE)
    strict = src < trow
    incl = src <= trow
    eye = (src == trow).astype(F32)
    bd_mask = ((lax.broadcasted_iota(jnp.int32, (pw2, pw2), 0) < HEAD_SIZE)
               == (lax.broadcasted_iota(jnp.int32, (pw2, pw2), 1) < HEAD_SIZE))

    crow = lax.broadcasted_iota(jnp.int32, (L, 1), 0)

    def bd(x):
        xb = x.astype(BF16)
        zero = jnp.zeros_like(xb)
        return jnp.concatenate([jnp.where(left, xb, zero), jnp.where(left, zero, xb)], axis=0)

    def cumsum_rows(x):
        sh = 1
        while sh < L:
            x = x + jnp.where(crow >= sh, pltpu.roll(x, sh, 0), 0.0)
            sh *= 2
        return x

    seqs = range(nseq)
    chunks = range(nch)
    blocks = [(q, j) for q in seqs for j in chunks]
    pairs = range(N_HEADS // 2)
    cs = [slice(p * pw2, (p + 1) * pw2) for p in pairs]
    units = [(c, p) for c in blocks for p in pairs]

    r, kmod, v, g, dec_last, a_hat, r_hat, b_hat, k_hat, b_til, k_til = ({} for _ in range(11))
    for c in blocks:
        q, j = c
        rs = slice(j * L, (j + 1) * L)
        lw = lw_ref[q, rs, :]
        p_inc = cumsum_rows(lw)
        p_last = p_inc[L - 1:L, :]
        e_p = jnp.exp(p_inc)
        e_np = jnp.exp(-p_inc)
        e_l = jnp.exp(p_last - p_inc)
        dec_last[c] = jnp.exp(p_last)
        r[c], kmod[c], v[c], bb, g[c] = (ref[q, rs, :].astype(F32) for ref in (r_ref, k_ref, v_ref, b_ref, g_ref))
        r_hat[c] = r[c] * e_p
        a_hat[c] = -kap_ref[q, rs, :].astype(F32) * jnp.exp(p_inc - lw)
        b_hat[c] = bb * e_np
        k_hat[c] = kmod[c] * e_np
        b_til[c] = bb * e_l
        k_til[c] = kmod[c] * e_l

    ar = {(c, p): jnp.concatenate([a_hat[c][:, cs[p]], r_hat[c][:, cs[p]]], axis=0).astype(BF16) for c, p in units}
    gram = {(c, p): _dot_nt(ar[c, p], jnp.concatenate([bd(b_hat[c][:, cs[p]]), bd(k_hat[c][:, cs[p]])], axis=0))
            for c, p in units}
    nab = {u_: jnp.where(strict, gram[u_][0:L, 0:pw2], 0.0) for u_ in units}
    nak = {u_: jnp.where(strict, gram[u_][0:L, pw2:2 * pw2], 0.0) for u_ in units}
    rbk = {u_: jnp.concatenate([jnp.where(incl, gram[u_][L:2 * L, 0:pw2], 0.0),
                                jnp.where(incl, gram[u_][L:2 * L, pw2:2 * pw2], 0.0)], axis=1).astype(BF16)
           for u_ in units}
    inv = {u_: eye + nab[u_] for u_ in units}
    pw = {u_: _dot(nab[u_], bd(nab[u_])) for u_ in units}
    for _ in range(4):
        px = {u_: _dot(jnp.concatenate([pw[u_], inv[u_]], axis=0), bd(pw[u_])) for u_ in units}
        pw = {u_: px[u_][0:L] for u_ in units}
        inv = {u_: inv[u_] + px[u_][L:2 * L] for u_ in units}
    inv = {u_: inv[u_] + _dot(inv[u_], bd(pw[u_])) for u_ in units}
    vp = {(c, p): v[c][:, cs[p]] for c, p in units}
    bdv = {u_: bd(vp[u_]) for u_ in units}
    akv = {u_: _dot(nak[u_], bdv[u_]) for u_ in units}

    lanes = [(q, p) for q in seqs for p in pairs]
    s_cur = {qp: s_ref[qp] for qp in lanes}
    y = {}
    for j in chunks:
        ars = {(q, p): _dot_nt(ar[(q, j), p], s_cur[q, p]) for q, p in lanes}
        u = {(q, p): _dot(inv[(q, j), p], bd(ars[q, p][0:L] + akv[(q, j), p])) for q, p in lanes}
        for q, p in lanes:
            y[(q, j), p] = ars[q, p][L:2 * L] + _dot(rbk[(q, j), p],
                                                     jnp.concatenate([bd(u[q, p]), bdv[(q, j), p]], axis=0))
        upd = {(q, p): _dot_tn(jnp.concatenate([u[q, p], vp[(q, j), p]], axis=0),
                               jnp.concatenate([b_til[q, j][:, cs[p]], k_til[q, j][:, cs[p]]], axis=0))
               for q, p in lanes}
        s_cur = {(q, p): s_cur[q, p] * dec_last[q, j][:, cs[p]] + jnp.where(bd_mask, upd[q, p], 0.0)
                 for q, p in lanes}
    for qp in lanes:
        s_ref[qp] = s_cur[qp]
        snew_ref[qp] = s_cur[qp]

    inv_n = 1.0 / HEAD_SIZE
    mpair = mpair_ref[...]
    for c in blocks:
        q, j = c
        y_all = jnp.concatenate([y[c, p] for p in pairs], axis=1)
        mu = _head_sum(y_all, mpair) * inv_n
        yc = y_all - mu
        var = _head_sum(yc * yc, mpair) * inv_n
        yn = yc * lax.rsqrt(var + GN_EPS) * gnw_ref[...] + gnb_ref[...]
        bonus = _head_sum(r[c] * kmod[c] * rk_ref[...], mpair) * v[c]
        yg_ref[q, j * L:(j + 1) * L, :] = ((yn + bonus) * g[c]).astype(BF16)


def _wkv_prompt(ops, rk, gnw, gnb, mpair, l, batch, seq):
    nch = math.gcd(seq // CHUNK, WKV_CHUNKS_PER_STEP)
    nseq = math.gcd(batch, WKV_SEQS_PER_STEP)
    rows_per_step = nch * CHUNK
    nc = seq // rows_per_step
    npair = N_HEADS // 2
    pw2 = 2 * HEAD_SIZE
    view = lambda a: a.reshape(batch // nseq, nseq, seq, D_RWKV)
    row_spec = pl.BlockSpec((None, nseq, rows_per_step, D_RWKV), lambda b, c: (b, 0, c, 0))
    yg, sbd = pl.pallas_call(
        functools.partial(_wkv_prompt_kernel, nch=nch, nseq=nseq),
        grid=(batch // nseq, nc),
        in_specs=[row_spec] * 7 + [_layer_spec((1, D_RWKV), l)] * 3 + [_const_spec((pw2, pw2))],
        out_specs=[row_spec, pl.BlockSpec((None, nseq, npair, pw2, pw2), lambda b, c: (b, 0, 0, 0, 0))],
        out_shape=[jax.ShapeDtypeStruct((batch // nseq, nseq, seq, D_RWKV), BF16),
                   jax.ShapeDtypeStruct((batch // nseq, nseq, npair, pw2, pw2), F32)],
        scratch_shapes=[pltpu.VMEM((nseq, npair, pw2, pw2), F32)],
        compiler_params=_params(("arbitrary", "arbitrary")),
    )(*[view(a) for a in ops], rk, gnw, gnb, mpair)
    yg = yg.reshape(batch * seq, D_RWKV)
    sbd = sbd.reshape(batch, npair, pw2, pw2)
    n = HEAD_SIZE
    wkv = jnp.stack([sbd[:, :, 0:n, 0:n], sbd[:, :, n:2 * n, n:2 * n]], axis=2)
    return yg, wkv.reshape(batch, N_HEADS, n, n)


def _wkv_sample_kernel(r_ref, k_ref, v_ref, kap_ref, b_ref, w_ref, g_ref, rk_ref, gnw_ref, gnb_ref, s_ref, *rest):
    *prev_refs, yg_ref, snew_ref, y_ref = rest
    slot = len(prev_refs)
    for j, prev_ref in enumerate(prev_refs):
        snew_ref[j] = prev_ref[...]
    kap, w, nb, kmod, r = kap_ref[...], w_ref[...], -b_ref[...], k_ref[...], r_ref[...]

    def body(vi, carry):
        row = pl.ds(vi, 1)
        s = s_ref[vi]
        sa = jnp.sum(s * kap, axis=0, keepdims=True)
        s_new = s * w + sa * nb + v_ref[row, :] * kmod
        snew_ref[slot, vi] = s_new
        y_ref[row, :] = jnp.sum(s_new * r, axis=0, keepdims=True)
        return carry

    lax.fori_loop(0, HEAD_SIZE, body, 0, unroll=SAMPLE_VALUE_UNROLL)

    y = y_ref[...]
    mu = jnp.mean(y, axis=0, keepdims=True)
    yc = y - mu
    var = jnp.mean(yc * yc, axis=0, keepdims=True)
    yn = yc * lax.rsqrt(var + GN_EPS) * gnw_ref[...] + gnb_ref[...]
    bonus = jnp.sum(r * kmod * rk_ref[...], axis=0, keepdims=True) * v_ref[...]
    yg_ref[...] = (yn + bonus) * g_ref[...]


def _wkv_sample(ops, rk, gnw, gnb, state_all, prev_new, l):
    depth, n = state_all.shape[0], state_all.shape[-1]
    prev = list(prev_new) if l == depth - 1 else []
    slots = len(prev) + 1
    hn = HEAD_SIZE
    col_spec = pl.BlockSpec((hn, n), lambda h: (h, 0))
    par_spec = pl.BlockSpec((None, hn, 1), lambda h: (l, h, 0))
    tile = (hn, hn, n)
    st_in = pl.BlockSpec((None, None) + tile, lambda h: (l, h, 0, 0, 0))
    prev_spec = pl.BlockSpec((None, None) + tile, lambda h: (0, h, 0, 0, 0))
    st_out = pl.BlockSpec((slots, None) + tile, lambda h: (0, h, 0, 0, 0))
    return pl.pallas_call(
        _wkv_sample_kernel,
        grid=(N_HEADS,),
        in_specs=[col_spec] * 7 + [par_spec] * 3 + [st_in] + [prev_spec] * len(prev),
        out_specs=[col_spec, st_out],
        out_shape=[jax.ShapeDtypeStruct((D_RWKV, n), F32),
                   jax.ShapeDtypeStruct((slots, N_HEADS) + tile, F32)],
        scratch_shapes=[pltpu.VMEM((hn, n), F32)],
        compiler_params=_params(("arbitrary",)),
    )(*ops, rk, gnw, gnb, state_all, *prev)


def _mix_out_kernel(yg_ref, gpool_ref, sgr_ref, x_ref, mod_ref, wbr_ref, wout_ref, lnw_ref, lnb_ref, o_ref, *, alpha,
                    yg_transposed, nsub):
    sub = o_ref.shape[0] // nsub
    rows = [slice(i * sub, (i + 1) * sub) for i in range(nsub)]
    if yg_transposed:
        yg = [yg_ref[...].T]
    else:
        yg = [yg_ref[rs, :] for rs in rows]
    y_rwkv = [_dot(yg[i], wbr_ref[...]) for i in range(nsub)]
    merged = [gpool_ref[rs, :].astype(F32) + sgr_ref[rs, :].astype(F32) * y_rwkv[i] for i, rs in enumerate(rows)]
    mix = [_dot(merged[i], wout_ref[...]) for i in range(nsub)]
    g1 = mod_ref[:, 2 * D_MODEL:3 * D_MODEL]
    for i, rs in enumerate(rows):
        o_ref[rs, :] = _ln(alpha * x_ref[rs, :] + (1.0 + g1) * mix[i]) * lnw_ref[...] + lnb_ref[...]


def _mix_out(yg, gpool, sgr, x2d, mod, mod_spec, grid, row_map, tm, wts, l, alpha, yg_transposed=False, nsub=1):
    row_spec = pl.BlockSpec((tm, D_MODEL), row_map)
    yg_spec = _const_spec(yg.shape) if yg_transposed else row_spec
    return pl.pallas_call(
        functools.partial(_mix_out_kernel, alpha=alpha, yg_transposed=yg_transposed, nsub=nsub),
        grid=grid,
        in_specs=[yg_spec] + [row_spec] * 3
        + [mod_spec, _layer_spec((D_RWKV, D_MODEL), l), _layer_spec((D_MODEL, D_MODEL), l),
           _layer_spec((1, D_MODEL), l), _layer_spec((1, D_MODEL), l)],
        out_specs=row_spec,
        out_shape=jax.ShapeDtypeStruct(x2d.shape, F32),
        compiler_params=_params(("arbitrary",) * len(grid)),
    )(yg, gpool, sgr, x2d, mod, *wts)


def _ffn_prompt_kernel(x_ref, mod_ref, up_ref, cw_ref, cb_ref, down_ref, lnw_ref, lnb_ref,
                       o_ref, conv_ref, carry_ref, *, sub, nsub, alpha):
    @pl.when(pl.program_id(1) == 0)
    def _():
        carry_ref[...] = jnp.zeros(carry_ref.shape, F32)

    rows = [slice(i * sub, (i + 1) * sub) for i in range(nsub)]
    row = lax.broadcasted_iota(jnp.int32, (sub, 1), 0)
    hb = []
    for rs in rows:
        h2 = _ln(x_ref[rs, :]) * (1.0 + mod_ref[:, 4 * D_MODEL:5 * D_MODEL]) + mod_ref[:, 3 * D_MODEL:4 * D_MODEL]
        hb.append(h2.astype(BF16))

    def conv(i, c0, c1):
        u = jnp.dot(hb[i], up_ref[:, c0:c1], preferred_element_type=F32)
        c6 = carry_ref[6:7, c0:c1]
        c7 = carry_ref[7:8, c0:c1]
        p1 = jnp.where(row == 0, c7, pltpu.roll(u, 1, 0))
        p2 = jnp.where(row == 0, c6, jnp.where(row == 1, c7, pltpu.roll(u, 2, 0)))
        carry_ref[6:8, c0:c1] = u[sub - 2:sub, :]
        if i == nsub - 1:
            conv_ref[:, c0:c1] = u[sub - 2:sub, :]
        return cb_ref[:, c0:c1] + p2 * cw_ref[0:1, c0:c1] + p1 * cw_ref[1:2, c0:c1] + u * cw_ref[2:3, c0:c1]

    nchunk = D_FF // FF_CHUNK
    act = []
    for i in range(nsub):
        val = [conv(i, c * FF_CHUNK, (c + 1) * FF_CHUNK) for c in range(nchunk)]
        gate = [conv(i, D_FF + c * FF_CHUNK, D_FF + (c + 1) * FF_CHUNK) for c in range(nchunk)]
        act.append([(val[c] * jax.nn.silu(gate[c])).astype(BF16) for c in range(nchunk)])
    ff = []
    for i in range(nsub):
        acc = jnp.zeros((sub, D_MODEL), F32)
        for c in range(nchunk):
            acc = acc + jnp.dot(act[i][c], down_ref[c * FF_CHUNK:(c + 1) * FF_CHUNK, :], preferred_element_type=F32)
        ff.append(acc)
    g2 = mod_ref[:, 5 * D_MODEL:6 * D_MODEL]
    for i, rs in enumerate(rows):
        o_ref[rs, :] = _ln(alpha * x_ref[rs, :] + (1.0 + g2) * ff[i]) * lnw_ref[...] + lnb_ref[...]


def _ffn_weight_specs(l):
    return [_layer_spec((D_MODEL, 2 * D_FF), l), _layer_spec((CONV_W, 2 * D_FF), l), _layer_spec((1, 2 * D_FF), l),
            _layer_spec((D_FF, D_MODEL), l), _layer_spec((1, D_MODEL), l), _layer_spec((1, D_MODEL), l)]


def _ffn_prompt(x2d, mod3, wts, l, batch, seq, alpha):
    sub = min(ROW_TILE, seq)
    nsub = math.gcd(seq // sub, FFN_SUBTILES)
    tm = sub * nsub
    nj = seq // tm
    row_spec = pl.BlockSpec((tm, D_MODEL), lambda b, j: (b * nj + j, 0))
    return pl.pallas_call(
        functools.partial(_ffn_prompt_kernel, sub=sub, nsub=nsub, alpha=alpha),
        grid=(batch, nj),
        in_specs=[row_spec, pl.BlockSpec((None, 1, 6 * D_MODEL), lambda b, j: (b, 0, 0))] + _ffn_weight_specs(l),
        out_specs=[row_spec, pl.BlockSpec((None, CONV_W - 1, 2 * D_FF), lambda b, j: (b, 0, 0))],
        out_shape=[jax.ShapeDtypeStruct(x2d.shape, F32),
                   jax.ShapeDtypeStruct((batch, CONV_W - 1, 2 * D_FF), F32)],
        scratch_shapes=[pltpu.VMEM((8, 2 * D_FF), F32)],
        compiler_params=_params(("arbitrary", "arbitrary")),
    )(x2d, mod3, *wts)


def _ffn_sample_kernel(x_ref, mod_ref, cst_ref, up_ref, cw_ref, cb_ref, down_ref, lnw_ref, lnb_ref,
                       o_ref, conv_ref, *, alpha):
    x = x_ref[...]
    h2 = _ln(x) * (1.0 + mod_ref[:, 4 * D_MODEL:5 * D_MODEL]) + mod_ref[:, 3 * D_MODEL:4 * D_MODEL]
    hb = h2.astype(BF16)

    def conv(c0, c1):
        u = jnp.dot(hb, up_ref[:, c0:c1], preferred_element_type=F32)
        p2 = cst_ref[:, 0, c0:c1]
        p1 = cst_ref[:, 1, c0:c1]
        conv_ref[:, 0, c0:c1] = p1
        conv_ref[:, 1, c0:c1] = u
        return cb_ref[:, c0:c1] + p2 * cw_ref[0:1, c0:c1] + p1 * cw_ref[1:2, c0:c1] + u * cw_ref[2:3, c0:c1]

    ff = jnp.zeros(x.shape, F32)
    for c in range(D_FF // FF_CHUNK):
        val = conv(c * FF_CHUNK, (c + 1) * FF_CHUNK)
        gate = conv(D_FF + c * FF_CHUNK, D_FF + (c + 1) * FF_CHUNK)
        ff = ff + _dot(val * jax.nn.silu(gate), down_ref[c * FF_CHUNK:(c + 1) * FF_CHUNK, :])
    g2 = mod_ref[:, 5 * D_MODEL:6 * D_MODEL]
    o_ref[...] = _ln(alpha * x + (1.0 + g2) * ff) * lnw_ref[...] + lnb_ref[...]


def _ffn_sample(x2d, mod2, conv_all, wts, l, alpha):
    n = x2d.shape[0]
    st = (n, CONV_W - 1, 2 * D_FF)
    out_full = lambda shape: pl.BlockSpec(shape, lambda i: (0,) * len(shape))
    return pl.pallas_call(
        functools.partial(_ffn_sample_kernel, alpha=alpha),
        grid=(1,),
        in_specs=[_const_spec((n, D_MODEL)), _const_spec((n, 6 * D_MODEL)), _layer_spec(st, l)]
        + _ffn_weight_specs(l),
        out_specs=[out_full((n, D_MODEL)), out_full(st)],
        out_shape=[jax.ShapeDtypeStruct(x2d.shape, F32), jax.ShapeDtypeStruct(st, F32)],
        compiler_params=_params(("arbitrary",)),
    )(x2d, mod2, conv_all, *wts)


def _stacked_weights(p):
    depth = p['w_in'].shape[0]
    row = lambda a: a.reshape(depth, 1, -1)
    bf = lambda a: a.astype(BF16)
    zeros = lambda r, c: jnp.zeros((depth, r, c), F32)
    lora_up = jnp.concatenate([
        jnp.concatenate([p['decay_up'], zeros(D_DECAY_LORA, 2 * D_RWKV)], axis=2),
        jnp.concatenate([zeros(D_AAA_LORA, D_RWKV), p['aaa_up'], zeros(D_AAA_LORA, D_RWKV)], axis=2),
        jnp.concatenate([zeros(D_GATE_LORA, 2 * D_RWKV), p['gate_up']], axis=2)], axis=1)
    head = jnp.arange(2 * HEAD_SIZE) // HEAD_SIZE
    mpair = (head[:, None] == head[None, :]).astype(BF16)
    mix_in = (bf(p['w_in']), row(p['mu_shift']), bf(lora_up), row(p['decay_w0']), row(p['aaa_a0']),
              row(p['k_k']), row(p['k_a']), mpair, bf(p['pool_w']), row(p['pool_scale']), bf(p['w_branch_pool']))
    wkv = (row(p['r_k']), row(p['gn_w']), row(p['gn_b']))
    col = lambda a: a.reshape(depth, -1, 1)
    wkv_cols = (col(p['r_k']), col(p['gn_w']), col(p['gn_b']))
    mix_out = (bf(p['w_branch_rwkv']), bf(p['w_out']), row(p['ln1_w']), row(p['ln1_b']))
    ffn = (bf(p['ffn_up']), p['conv_w'], row(p['conv_b']), bf(p['ffn_down']), row(p['ln2_w']), row(p['ln2_b']))
    return mix_in, wkv, wkv_cols, mix_out, ffn


def _trunk(x_prompt, x_sample, state_shift, state_wkv, state_pool, state_conv, mod, p):
    depth = p['w_in'].shape[0]
    alpha = float((2 * depth) ** 0.25)
    batch, seq, _ = x_prompt.shape
    nsamp = x_sample.shape[0]
    xp = x_prompt.reshape(batch * seq, D_MODEL).astype(F32)
    xs = x_sample.reshape(nsamp, D_MODEL).astype(F32)
    tm = min(ROW_TILE, seq)
    nj = seq // tm
    mo_sub = math.gcd(nj, MIX_OUT_SUBTILES)
    prompt_mod_spec = pl.BlockSpec((None, 1, 6 * D_MODEL), lambda i: (i // (nj // mo_sub), 0, 0))
    sample_mod_spec = pl.BlockSpec((nsamp, 6 * D_MODEL), lambda i: (0, 0))
    w_mix_in, w_wkv, w_wkv_cols, w_mix_out, w_ffn = _stacked_weights(p)
    shift_all = state_shift.astype(F32)
    wkv_all = jnp.transpose(state_wkv.astype(F32), (0, 2, 3, 4, 1))
    pool_all = state_pool.astype(F32)
    conv_all = state_conv.astype(F32)
    outs = {k: [] for k in ('shift_p', 'wkv_p', 'pool_p', 'conv_p', 'shift_s', 'wkv_s', 'pool_s', 'conv_s')}
    for l in range(depth):
        mod_p = mod[l, 0:batch].reshape(batch, 1, 6 * D_MODEL)
        mod_s = mod[l, batch:batch + nsamp]

        *ops, gpool, sgr, shift_p, pool_p = _mix_in_prompt(xp, mod_p, w_mix_in, l, batch, seq)
        yg, wkv_p = _wkv_prompt(ops, *w_wkv, w_mix_in[7], l, batch, seq)
        xp = _mix_out(yg, gpool, sgr, xp, mod_p, prompt_mod_spec, (batch * nj // mo_sub,), lambda i: (i, 0),
                      tm * mo_sub, w_mix_out, l, alpha, nsub=mo_sub)
        xp, conv_p = _ffn_prompt(xp, mod_p, w_ffn, l, batch, seq, alpha)
        outs['shift_p'].append(shift_p.reshape(batch, D_MODEL))
        outs['wkv_p'].append(wkv_p)
        outs['pool_p'].append(pool_p[:, 1:, :])
        outs['conv_p'].append(conv_p)

        *ops, gpool, sgr, shift_s, pool_s = _mix_in_sample(xs, mod_s, shift_all, pool_all, w_mix_in, l)
        yg, wkv_s = _wkv_sample(ops, *w_wkv_cols, wkv_all, outs['wkv_s'], l)
        xs = _mix_out(yg, gpool, sgr, xs, mod_s, sample_mod_spec, (1,), lambda i: (0, 0), nsamp, w_mix_out, l,
                      alpha, yg_transposed=True)
        xs, conv_s = _ffn_sample(xs, mod_s, conv_all, w_ffn, l, alpha)
        outs['shift_s'].append(shift_s)
        outs['wkv_s'].append(wkv_s)
        outs['pool_s'].append(pool_s)
        outs['conv_s'].append(conv_s)
    wkv_s_all = jnp.transpose(outs.pop('wkv_s')[-1], (0, 4, 1, 2, 3))
    st = {k: jnp.stack(v) for k, v in outs.items()}
    return (xp.reshape(batch, seq, D_MODEL), xs.reshape(nsamp, 1, D_MODEL), st['shift_p'], st['wkv_p'], st['pool_p'],
            st['conv_p'], st['shift_s'], wkv_s_all, st['pool_s'], st['conv_s'])


def kernel(x_prompt, x_sample, state_shift, state_wkv, state_pool, state_conv, c_prompt, c_sample, ada_w, ada_b, w_in, mu_shift, decay_w0, decay_up, aaa_a0, aaa_up, gate_up, k_k, k_a, r_k, gn_w, gn_b, w_branch_rwkv, pool_w, pool_scale, w_branch_pool, w_out, ln1_w, ln1_b, ffn_up, conv_w, conv_b, ffn_down, ln2_w, ln2_b):
    p = {'w_in': w_in, 'mu_shift': mu_shift, 'decay_w0': decay_w0, 'decay_up': decay_up, 'aaa_a0': aaa_a0,
         'aaa_up': aaa_up, 'gate_up': gate_up, 'k_k': k_k, 'k_a': k_a, 'r_k': r_k, 'gn_w': gn_w, 'gn_b': gn_b,
         'w_branch_rwkv': w_branch_rwkv, 'pool_w': pool_w, 'pool_scale': pool_scale,
         'w_branch_pool': w_branch_pool, 'w_out': w_out, 'ln1_w': ln1_w, 'ln1_b': ln1_b, 'ffn_up': ffn_up,
         'conv_w': conv_w, 'conv_b': conv_b, 'ffn_down': ffn_down, 'ln2_w': ln2_w, 'ln2_b': ln2_b}
    c_all = jnp.concatenate([c_prompt, c_sample], axis=0).astype(F32)
    mod = _modulation(c_all, ada_w, ada_b)
    dt = x_prompt.dtype
    out = _trunk(x_prompt, x_sample, state_shift, state_wkv, state_pool, state_conv, mod, p)
    return tuple(o.astype(dt) for o in out)
```

```python
import functools
import math

import jax
import jax.numpy as jnp
from jax import lax
from jax.experimental import pallas as pl
from jax.experimental.pallas import tpu as pltpu

F32 = jnp.float32
BF16 = jnp.bfloat16

D_MODEL = 1024
HEAD_SIZE = 64
N_HEADS = D_MODEL // HEAD_SIZE
D_RWKV = N_HEADS * HEAD_SIZE
D_DECAY_LORA = 64
D_AAA_LORA = 64
D_GATE_LORA = 128
D_LORA = D_DECAY_LORA + D_AAA_LORA + D_GATE_LORA
POOL_WINDOWS = (2, 4, 8, 16)
POOL_GROUP = D_MODEL // 8
D_POOL = len(POOL_WINDOWS) * POOL_GROUP
POOL_BUF = max(POOL_WINDOWS) - 1
D_FF = (D_MODEL * 11) // 4
CONV_W = 3
D_SHIFT = 3 * D_RWKV + D_LORA
IN_COLS = D_SHIFT + D_POOL + 2 * D_MODEL
LN_EPS = 1e-5
GN_EPS = 64e-5
EXP_M05 = math.exp(-0.5)

VMEM_LIMIT_BYTES = 56 * 1024 * 1024
ROW_TILE = 256
CHUNK = 64
MIX_IN_SUBTILES = 2
MIX_OUT_SUBTILES = 4
FFN_SUBTILES = 4
WKV_CHUNKS_PER_STEP = 2
WKV_SEQS_PER_STEP = 2
SAMPLE_VALUE_UNROLL = 4
FF_CHUNK = 2816


def _ln(x):
    mu = jnp.mean(x, axis=-1, keepdims=True)
    xc = x - mu
    var = jnp.mean(xc * xc, axis=-1, keepdims=True)
    return xc * lax.rsqrt(var + LN_EPS)


def _dot(a, b):
    return jnp.dot(a.astype(BF16), b.astype(BF16), preferred_element_type=F32)


def _dot_nt(a, b):
    return lax.dot_general(a.astype(BF16), b.astype(BF16), (((1,), (1,)), ((), ())),
                           preferred_element_type=F32)


def _dot_tn(a, b):
    return lax.dot_general(a.astype(BF16), b.astype(BF16), (((0,), (0,)), ((), ())),
                           preferred_element_type=F32)


def _const_spec(shape):
    nd = len(shape)
    return pl.BlockSpec(shape, lambda *_: (0,) * nd, pipeline_mode=pl.Buffered(1))


def _layer_spec(shape, l):
    nd = len(shape)
    return pl.BlockSpec((None,) + tuple(shape), lambda *_: (l,) + (0,) * nd, pipeline_mode=pl.Buffered(1))


def _params(sem):
    return pltpu.CompilerParams(dimension_semantics=sem, vmem_limit_bytes=VMEM_LIMIT_BYTES)


def _mod_kernel(c_ref, w_ref, b_ref, o_ref):
    o_ref[...] = _dot(c_ref[...], w_ref[...]) + b_ref[...]


def _modulation(c_all, ada_w, ada_b):
    depth, d, n = ada_w.shape
    rows = c_all.shape[0]
    tn = 1536
    return pl.pallas_call(
        _mod_kernel,
        grid=(depth, n // tn),
        in_specs=[pl.BlockSpec((rows, d), lambda l, j: (0, 0)),
                  pl.BlockSpec((None, d, tn), lambda l, j: (l, 0, j)),
                  pl.BlockSpec((None, 1, tn), lambda l, j: (l, 0, j))],
        out_specs=pl.BlockSpec((None, rows, tn), lambda l, j: (l, 0, j)),
        out_shape=jax.ShapeDtypeStruct((depth, rows, n), F32),
        compiler_params=_params(("arbitrary", "arbitrary")),
    )(c_all, ada_w, ada_b.reshape(depth, 1, n))


def _head_sum(x, mpair):
    rows = x.shape[0]
    tiles = range(x.shape[1] // (2 * HEAD_SIZE))
    st = jnp.concatenate([x[:, t * 2 * HEAD_SIZE:(t + 1) * 2 * HEAD_SIZE] for t in tiles], axis=0)
    sm = _dot(st, mpair)
    return jnp.concatenate([sm[t * rows:(t + 1) * rows] for t in tiles], axis=1)


def _rwkv_prep_dots(xs_k, lora, wl_ref, kk_ref, mpair_ref):
    kk = xs_k * kk_ref[...]
    ss = _head_sum(kk * kk, mpair_ref[...])
    lane = lax.broadcasted_iota(jnp.int32, lora.shape, 1)
    act = jnp.where(lane < D_DECAY_LORA, jnp.tanh(lora),
                    jnp.where(lane < D_DECAY_LORA + D_AAA_LORA, lora, jax.nn.sigmoid(lora)))
    z = _dot(act, wl_ref[...])
    return z, kk, ss


def _rwkv_prep_finish(xs_r, xs_k, xs_v, z, kk, ss, w0_ref, a0_ref, ka_ref):
    dec = w0_ref[...] + z[:, 0:D_RWKV]
    lw = -jax.nn.sigmoid(dec) * EXP_M05
    a = jax.nn.sigmoid(a0_ref[...] + z[:, D_RWKV:2 * D_RWKV])
    g = z[:, 2 * D_RWKV:3 * D_RWKV]
    kap = kk / jnp.maximum(jnp.sqrt(ss), 1e-12)
    kmod = xs_k * (1.0 + (a - 1.0) * ka_ref[...])
    return xs_r, kmod, xs_v, kap, kap * a, lw, g


def _pool_mix(pooled, p_in, poolw_ref, pscale_ref, wbp_ref):
    mixed = []
    for gi in range(len(POOL_WINDOWS)):
        sl = slice(gi * POOL_GROUP, (gi + 1) * POOL_GROUP)
        diff = pooled[gi] - p_in[:, sl]
        mixed.append(_dot(diff, poolw_ref[gi]) * pscale_ref[:, sl])
    return _dot(jnp.concatenate(mixed, axis=-1), wbp_ref[...])


def _mix_in_prompt_kernel(x_ref, mod_ref, win_ref, mu_ref, wl_ref, w0_ref, a0_ref, kk_ref, ka_ref,
                          mpair_ref, poolw_ref, pscale_ref, wbp_ref,
                          r_ref, k_ref, v_ref, kap_ref, b_ref, lw_ref, g_ref, gpool_ref, sgr_ref,
                          shift_ref, pool_ref,
                          carry_ref, pext_ref, *, sub, nsub):
    j = pl.program_id(1)
    tm = sub * nsub

    @pl.when(j == 0)
    def _():
        carry_ref[...] = jnp.zeros(carry_ref.shape, F32)
        pext_ref[...] = jnp.zeros(pext_ref.shape, F32)

    tiles = range(nsub)
    rows = [slice(i * sub, (i + 1) * sub) for i in tiles]
    row = lax.broadcasted_iota(jnp.int32, (sub, 1), 0)
    hb = []
    for i in tiles:
        h = _ln(x_ref[rows[i], :]) * (1.0 + mod_ref[:, D_MODEL:2 * D_MODEL]) + mod_ref[:, 0:D_MODEL]
        if i == nsub - 1:
            shift_ref[...] = h[sub - 1:sub, :]
        hb.append(h.astype(BF16))

    def shifted(i, c0, c1):
        cur = jnp.dot(hb[i], win_ref[:, c0:c1], preferred_element_type=F32)
        prev = jnp.where(row == 0, carry_ref[7:8, c0:c1], pltpu.roll(cur, 1, 0))
        carry_ref[7:8, c0:c1] = cur[sub - 1:sub, :]
        return cur + (prev - cur) * mu_ref[:, c0:c1]

    xs_k = [shifted(i, D_RWKV, 2 * D_RWKV) for i in tiles]
    lora = [shifted(i, 3 * D_RWKV, D_SHIFT) for i in tiles]
    prep = [_rwkv_prep_dots(xs_k[i], lora[i], wl_ref, kk_ref, mpair_ref) for i in tiles]
    p_in = [jnp.dot(hb[i], win_ref[:, D_SHIFT:D_SHIFT + D_POOL], preferred_element_type=F32) for i in tiles]
    gates = [jnp.dot(hb[i], win_ref[:, D_SHIFT + D_POOL:IN_COLS], preferred_element_type=F32) for i in tiles]

    pext_ref[0:16, :] = pext_ref[tm:tm + 16, :]
    for i in tiles:
        pext_ref[16 + i * sub:16 + (i + 1) * sub, :] = p_in[i]
    pool_ref[...] = pext_ref[tm:tm + 16, :]
    y_pool = []
    for i in tiles:
        base = 16 + i * sub
        pos1 = (j * tm + i * sub + row + 1).astype(F32)
        pooled = []
        for gi, win in enumerate(POOL_WINDOWS):
            sl = slice(gi * POOL_GROUP, (gi + 1) * POOL_GROUP)
            acc = p_in[i][:, sl]
            for d in range(1, win):
                acc = acc + pext_ref[base - d:base - d + sub, sl]
            pooled.append(acc / jnp.minimum(float(win), pos1))
        y_pool.append(_pool_mix(pooled, p_in[i], poolw_ref, pscale_ref, wbp_ref))

    xs_r = [shifted(i, 0, D_RWKV) for i in tiles]
    xs_v = [shifted(i, 2 * D_RWKV, 3 * D_RWKV) for i in tiles]
    for i in tiles:
        rs = rows[i]
        r, kmod, v, kap, b, lw, g = _rwkv_prep_finish(xs_r[i], xs_k[i], xs_v[i], *prep[i], w0_ref, a0_ref, ka_ref)
        r_ref[rs, :] = r.astype(BF16)
        k_ref[rs, :] = kmod.astype(BF16)
        v_ref[rs, :] = v.astype(BF16)
        kap_ref[rs, :] = kap.astype(BF16)
        b_ref[rs, :] = b.astype(BF16)
        lw_ref[rs, :] = lw
        g_ref[rs, :] = g.astype(BF16)
        sgr_ref[rs, :] = jax.nn.sigmoid(gates[i][:, D_MODEL:2 * D_MODEL]).astype(BF16)
        gpool_ref[rs, :] = (jax.nn.sigmoid(gates[i][:, 0:D_MODEL]) * y_pool[i]).astype(BF16)


def _mix_in_sample_kernel(x_ref, mod_ref, sprev_ref, pst_ref, win_ref, mu_ref, wl_ref, w0_ref, a0_ref, kk_ref,
                          ka_ref, mpair_ref, poolw_ref, pscale_ref, wbp_ref,
                          r_ref, k_ref, v_ref, kap_ref, b_ref, w_ref, g_ref, gpool_ref, sgr_ref,
                          shift_ref, pool_ref):
    x = x_ref[...]
    h = _ln(x) * (1.0 + mod_ref[:, D_MODEL:2 * D_MODEL]) + mod_ref[:, 0:D_MODEL]
    shift_ref[...] = h
    hb = h.astype(BF16)
    pb = sprev_ref[...].astype(BF16)

    def shifted(c0, c1):
        w = win_ref[:, c0:c1]
        cur = jnp.dot(hb, w, preferred_element_type=F32)
        prev = jnp.dot(pb, w, preferred_element_type=F32)
        return cur + (prev - cur) * mu_ref[:, c0:c1]

    xs_r = shifted(0, D_RWKV)
    xs_k = shifted(D_RWKV, 2 * D_RWKV)
    xs_v = shifted(2 * D_RWKV, 3 * D_RWKV)
    lora = shifted(3 * D_RWKV, D_SHIFT)
    z, kk, ss = _rwkv_prep_dots(xs_k, lora, wl_ref, kk_ref, mpair_ref)
    r, kmod, v, kap, b, lw, g = _rwkv_prep_finish(xs_r, xs_k, xs_v, z, kk, ss, w0_ref, a0_ref, ka_ref)
    r_ref[...] = r.T
    k_ref[...] = kmod.T
    v_ref[...] = v.T
    kap_ref[...] = kap.T
    b_ref[...] = b.T
    w_ref[...] = jnp.exp(lw).T
    g_ref[...] = g.T

    p_in = jnp.dot(hb, win_ref[:, D_SHIFT:D_SHIFT + D_POOL], preferred_element_type=F32)
    pooled = []
    for gi, win in enumerate(POOL_WINDOWS):
        sl = slice(gi * POOL_GROUP, (gi + 1) * POOL_GROUP)
        acc = p_in[:, sl]
        for d in range(1, win):
            acc = acc + pst_ref[POOL_BUF - d, :, sl]
        pooled.append(acc / float(win))
    gates = jnp.dot(hb, win_ref[:, D_SHIFT + D_POOL:IN_COLS], preferred_element_type=F32)
    gpool = jax.nn.sigmoid(gates[:, 0:D_MODEL]) * _pool_mix(pooled, p_in, poolw_ref, pscale_ref, wbp_ref)
    gpool_ref[...] = gpool.astype(BF16)
    sgr_ref[...] = jax.nn.sigmoid(gates[:, D_MODEL:2 * D_MODEL]).astype(BF16)
    for i in range(POOL_BUF - 1):
        pool_ref[i] = pst_ref[i + 1]
    pool_ref[POOL_BUF - 1] = p_in


def _mix_in_weight_specs(l):
    return [_layer_spec((D_MODEL, IN_COLS), l),
            _layer_spec((1, D_SHIFT), l),
            _layer_spec((D_LORA, 3 * D_RWKV), l),
            _layer_spec((1, D_RWKV), l),
            _layer_spec((1, D_RWKV), l),
            _layer_spec((1, D_RWKV), l),
            _layer_spec((1, D_RWKV), l),
            _const_spec((2 * HEAD_SIZE, 2 * HEAD_SIZE)),
            _layer_spec((len(POOL_WINDOWS), POOL_GROUP, POOL_GROUP), l),
            _layer_spec((1, D_POOL), l),
            _layer_spec((D_POOL, D_MODEL), l)]


def _mix_in_prompt(x2d, mod3, wts, l, batch, seq):
    sub = min(ROW_TILE, seq)
    nsub = math.gcd(seq // sub, MIX_IN_SUBTILES)
    tm = sub * nsub
    nj = seq // tm
    rows = batch * seq
    row_spec = pl.BlockSpec((tm, D_MODEL), lambda b, j: (b * nj + j, 0))
    f32_rows = jax.ShapeDtypeStruct((rows, D_MODEL), F32)
    bf_rows = jax.ShapeDtypeStruct((rows, D_MODEL), BF16)
    return pl.pallas_call(
        functools.partial(_mix_in_prompt_kernel, sub=sub, nsub=nsub),
        grid=(batch, nj),
        in_specs=[row_spec, pl.BlockSpec((None, 1, 6 * D_MODEL), lambda b, j: (b, 0, 0))] + _mix_in_weight_specs(l),
        out_specs=[row_spec] * 9 + [pl.BlockSpec((None, 1, D_MODEL), lambda b, j: (b, 0, 0)),
                                    pl.BlockSpec((None, 16, D_POOL), lambda b, j: (b, 0, 0))],
        out_shape=[bf_rows] * 5 + [f32_rows] + [bf_rows] * 3 + [jax.ShapeDtypeStruct((batch, 1, D_MODEL), F32),
                                                                jax.ShapeDtypeStruct((batch, 16, D_POOL), F32)],
        scratch_shapes=[pltpu.VMEM((8, D_SHIFT), F32), pltpu.VMEM((tm + 16, D_POOL), F32)],
        compiler_params=_params(("arbitrary", "arbitrary")),
    )(x2d, mod3, *wts)


def _mix_in_sample(x2d, mod2, shift_all, pool_all, wts, l):
    n = x2d.shape[0]
    out_full = lambda shape: pl.BlockSpec(shape, lambda i: (0,) * len(shape))
    f32_rows = jax.ShapeDtypeStruct((n, D_MODEL), F32)
    f32_cols = jax.ShapeDtypeStruct((D_RWKV, n), F32)
    bf_rows = jax.ShapeDtypeStruct((n, D_MODEL), BF16)
    return pl.pallas_call(
        _mix_in_sample_kernel,
        grid=(1,),
        in_specs=[_const_spec((n, D_MODEL)), _const_spec((n, 6 * D_MODEL)), _layer_spec((n, D_MODEL), l),
                  _layer_spec((POOL_BUF, n, D_POOL), l)] + _mix_in_weight_specs(l),
        out_specs=[out_full((D_RWKV, n))] * 7 + [out_full((n, D_MODEL))] * 3 + [out_full((POOL_BUF, n, D_POOL))],
        out_shape=[f32_cols] * 7 + [bf_rows] * 2 + [f32_rows, jax.ShapeDtypeStruct((POOL_BUF, n, D_POOL), F32)],
        compiler_params=_params(("arbitrary",)),
    )(x2d, mod2, shift_all, pool_all, *wts)


def _wkv_prompt_kernel(r_ref, k_ref, v_ref, kap_ref, b_ref, lw_ref, g_ref, rk_ref, gnw_ref, gnb_ref, mpair_ref,
                       yg_ref, snew_ref, s_ref, *, nch, nseq):
    @pl.when(pl.program_id(1) == 0)
    def _():
        s_ref[...] = jnp.zeros(s_ref.shape, F32)

    L = CHUNK
    pw2 = 2 * HEAD_SIZE
    lane = lax.broadcasted_iota(jnp.int32, (L, pw2), 1)
    trow = lax.broadcasted_iota(jnp.int32, (L, pw2), 0)
    left = lane < HEAD_SIZE
    src = jnp.where(left, lane, lane - HEAD_SIZE)
    strict = src < trow
    incl = src <= trow
    eye = (src == trow).astype(F32)
    bd_mask = ((lax.broadcasted_iota(jnp.int32, (pw2, pw2), 0) < HEAD_SIZE)
               == (lax.broadcasted_iota(jnp.int32, (pw2, pw2), 1) < HEAD_SIZE))

    crow = lax.broadcasted_iota(jnp.int32, (L, 1), 0)

    def bd(x):
        xb = x.astype(BF16)
        zero = jnp.zeros_like(xb)
        return jnp.concatenate([jnp.where(left, xb, zero), jnp.where(left, zero, xb)], axis=0)

    def cumsum_rows(x):
        sh = 1
        while sh < L:
            x = x + jnp.where(crow >= sh, pltpu.roll(x, sh, 0), 0.0)
            sh *= 2
        return x

    seqs = range(nseq)
    chunks = range(nch)
    blocks = [(q, j) for q in seqs for j in chunks]
    pairs = range(N_HEADS // 2)
    cs = [slice(p * pw2, (p + 1) * pw2) for p in pairs]
    units = [(c, p) for c in blocks for p in pairs]

    r, kmod, v, g, dec_last, a_hat, r_hat, b_hat, k_hat, b_til, k_til = ({} for _ in range(11))
    for c in blocks:
        q, j = c
        rs = slice(j * L, (j + 1) * L)
        lw = lw_ref[q, rs, :]
        p_inc = cumsum_rows(lw)
        p_last = p_inc[L - 1:L, :]
        e_p = jnp.exp(p_inc)
        e_np = jnp.exp(-p_inc)
        e_l = jnp.exp(p_last - p_inc)
        dec_last[c] = jnp.exp(p_last)
        r[c], kmod[c], v[c], bb, g[c] = (ref[q, rs, :].astype(F32) for ref in (r_ref, k_ref, v_ref, b_ref, g_ref))
        r_hat[c] = r[c] * e_p
        a_hat[c] = -kap_ref[q, rs, :].astype(F32) * jnp.exp(p_inc - lw)
        b_hat[c] = bb * e_np
        k_hat[c] = kmod[c] * e_np
        b_til[c] = bb * e_l
        k_til[c] = kmod[c] * e_l

    ar = {(c, p): jnp.concatenate([a_hat[c][:, cs[p]], r_hat[c][:, cs[p]]], axis=0).astype(BF16) for c, p in units}
    gram = {(c, p): _dot_nt(ar[c, p], jnp.concatenate([bd(b_hat[c][:, cs[p]]), bd(k_hat[c][:, cs[p]])], axis=0))
            for c, p in units}
    nab = {u_: jnp.where(strict, gram[u_][0:L, 0:pw2], 0.0) for u_ in units}
    nak = {u_: jnp.where(strict, gram[u_][0:L, pw2:2 * pw2], 0.0) for u_ in units}
    rbk = {u_: jnp.concatenate([jnp.where(incl, gram[u_][L:2 * L, 0:pw2], 0.0),
                                jnp.where(incl, gram[u_][L:2 * L, pw2:2 * pw2], 0.0)], axis=1).astype(BF16)
           for u_ in units}
    inv = {u_: eye + nab[u_] for u_ in units}
    pw = {u_: _dot(nab[u_], bd(nab[u_])) for u_ in units}
    for _ in range(4):
        px = {u_: _dot(jnp.concatenate([pw[u_], inv[u_]], axis=0), bd(pw[u_])) for u_ in units}
        pw = {u_: px[u_][0:L] for u_ in units}
        inv = {u_: inv[u_] + px[u_][L:2 * L] for u_ in units}
    inv = {u_: inv[u_] + _dot(inv[u_], bd(pw[u_])) for u_ in units}
    vp = {(c, p): v[c][:, cs[p]] for c, p in units}
    bdv = {u_: bd(vp[u_]) for u_ in units}
    akv = {u_: _dot(nak[u_], bdv[u_]) for u_ in units}

    lanes = [(q, p) for q in seqs for p in pairs]
    s_cur = {qp: s_ref[qp] for qp in lanes}
    y = {}
    for j in chunks:
        ars = {(q, p): _dot_nt(ar[(q, j), p], s_cur[q, p]) for q, p in lanes}
        u = {(q, p): _dot(inv[(q, j), p], bd(ars[q, p][0:L] + akv[(q, j), p])) for q, p in lanes}
        for q, p in lanes:
            y[(q, j), p] = ars[q, p][L:2 * L] + _dot(rbk[(q, j), p],
                                                     jnp.concatenate([bd(u[q, p]), bdv[(q, j), p]], axis=0))
        upd = {(q, p): _dot_tn(jnp.concatenate([u[q, p], vp[(q, j), p]], axis=0),
                               jnp.concatenate([b_til[q, j][:, cs[p]], k_til[q, j][:, cs[p]]], axis=0))
               for q, p in lanes}
        s_cur = {(q, p): s_cur[q, p] * dec_last[q, j][:, cs[p]] + jnp.where(bd_mask, upd[q, p], 0.0)
                 for q, p in lanes}
    for qp in lanes:
        s_ref[qp] = s_cur[qp]
        snew_ref[qp] = s_cur[qp]

    inv_n = 1.0 / HEAD_SIZE
    mpair = mpair_ref[...]
    for c in blocks:
        q, j = c
        y_all = jnp.concatenate([y[c, p] for p in pairs], axis=1)
        mu = _head_sum(y_all, mpair) * inv_n
        yc = y_all - mu
        var = _head_sum(yc * yc, mpair) * inv_n
        yn = yc * lax.rsqrt(var + GN_EPS) * gnw_ref[...] + gnb_ref[...]
        bonus = _head_sum(r[c] * kmod[c] * rk_ref[...], mpair) * v[c]
        yg_ref[q, j * L:(j + 1) * L, :] = ((yn + bonus) * g[c]).astype(BF16)


def _wkv_prompt(ops, rk, gnw, gnb, mpair, l, batch, seq):
    nch = math.gcd(seq // CHUNK, WKV_CHUNKS_PER_STEP)
    nseq = math.gcd(batch, WKV_SEQS_PER_STEP)
    rows_per_step = nch * CHUNK
    nc = seq // rows_per_step
    npair = N_HEADS // 2
    pw2 = 2 * HEAD_SIZE
    view = lambda a: a.reshape(batch // nseq, nseq, seq, D_RWKV)
    row_spec = pl.BlockSpec((None, nseq, rows_per_step, D_RWKV), lambda b, c: (b, 0, c, 0))
    yg, sbd = pl.pallas_call(
        functools.partial(_wkv_prompt_kernel, nch=nch, nseq=nseq),
        grid=(batch // nseq, nc),
        in_specs=[row_spec] * 7 + [_layer_spec((1, D_RWKV), l)] * 3 + [_const_spec((pw2, pw2))],
        out_specs=[row_spec, pl.BlockSpec((None, nseq, npair, pw2, pw2), lambda b, c: (b, 0, 0, 0, 0))],
        out_shape=[jax.ShapeDtypeStruct((batch // nseq, nseq, seq, D_RWKV), BF16),
                   jax.ShapeDtypeStruct((batch // nseq, nseq, npair, pw2, pw2), F32)],
        scratch_shapes=[pltpu.VMEM((nseq, npair, pw2, pw2), F32)],
        compiler_params=_params(("arbitrary", "arbitrary")),
    )(*[view(a) for a in ops], rk, gnw, gnb, mpair)
    yg = yg.reshape(batch * seq, D_RWKV)
    sbd = sbd.reshape(batch, npair, pw2, pw2)
    n = HEAD_SIZE
    wkv = jnp.stack([sbd[:, :, 0:n, 0:n], sbd[:, :, n:2 * n, n:2 * n]], axis=2)
    return yg, wkv.reshape(batch, N_HEADS, n, n)


def _wkv_sample_kernel(r_ref, k_ref, v_ref, kap_ref, b_ref, w_ref, g_ref, rk_ref, gnw_ref, gnb_ref, s_ref, *rest):
    *prev_refs, yg_ref, snew_ref, y_ref = rest
    slot = len(prev_refs)
    for j, prev_ref in enumerate(prev_refs):
        snew_ref[j] = prev_ref[...]
    kap, w, nb, kmod, r = kap_ref[...], w_ref[...], -b_ref[...], k_ref[...], r_ref[...]

    def body(vi, carry):
        row = pl.ds(vi, 1)
        s = s_ref[vi]
        sa = jnp.sum(s * kap, axis=0, keepdims=True)
        s_new = s * w + sa * nb + v_ref[row, :] * kmod
        snew_ref[slot, vi] = s_new
        y_ref[row, :] = jnp.sum(s_new * r, axis=0, keepdims=True)
        return carry

    lax.fori_loop(0, HEAD_SIZE, body, 0, unroll=SAMPLE_VALUE_UNROLL)

    y = y_ref[...]
    mu = jnp.mean(y, axis=0, keepdims=True)
    yc = y - mu
    var = jnp.mean(yc * yc, axis=0, keepdims=True)
    yn = yc * lax.rsqrt(var + GN_EPS) * gnw_ref[...] + gnb_ref[...]
    bonus = jnp.sum(r * kmod * rk_ref[...], axis=0, keepdims=True) * v_ref[...]
    yg_ref[...] = (yn + bonus) * g_ref[...]


def _wkv_sample(ops, rk, gnw, gnb, state_all, prev_new, l):
    depth, n = state_all.shape[0], state_all.shape[-1]
    prev = list(prev_new) if l == depth - 1 else []
    slots = len(prev) + 1
    hn = HEAD_SIZE
    col_spec = pl.BlockSpec((hn, n), lambda h: (h, 0))
    par_spec = pl.BlockSpec((None, hn, 1), lambda h: (l, h, 0))
    tile = (hn, hn, n)
    st_in = pl.BlockSpec((None, None) + tile, lambda h: (l, h, 0, 0, 0))
    prev_spec = pl.BlockSpec((None, None) + tile, lambda h: (0, h, 0, 0, 0))
    st_out = pl.BlockSpec((slots, None) + tile, lambda h: (0, h, 0, 0, 0))
    return pl.pallas_call(
        _wkv_sample_kernel,
        grid=(N_HEADS,),
        in_specs=[col_spec] * 7 + [par_spec] * 3 + [st_in] + [prev_spec] * len(prev),
        out_specs=[col_spec, st_out],
        out_shape=[jax.ShapeDtypeStruct((D_RWKV, n), F32),
                   jax.ShapeDtypeStruct((slots, N_HEADS) + tile, F32)],
        scratch_shapes=[pltpu.VMEM((hn, n), F32)],
        compiler_params=_params(("arbitrary",)),
    )(*ops, rk, gnw, gnb, state_all, *prev)


def _mix_out_kernel(yg_ref, gpool_ref, sgr_ref, x_ref, mod_ref, wbr_ref, wout_ref, lnw_ref, lnb_ref, o_ref, *, alpha,
                    yg_transposed, nsub):
    sub = o_ref.shape[0] // nsub
    rows = [slice(i * sub, (i + 1) * sub) for i in range(nsub)]
    if yg_transposed:
        yg = [yg_ref[...].T]
    else:
        yg = [yg_ref[rs, :] for rs in rows]
    y_rwkv = [_dot(yg[i], wbr_ref[...]) for i in range(nsub)]
    merged = [gpool_ref[rs, :].astype(F32) + sgr_ref[rs, :].astype(F32) * y_rwkv[i] for i, rs in enumerate(rows)]
    mix = [_dot(merged[i], wout_ref[...]) for i in range(nsub)]
    g1 = mod_ref[:, 2 * D_MODEL:3 * D_MODEL]
    for i, rs in enumerate(rows):
        o_ref[rs, :] = _ln(alpha * x_ref[rs, :] + (1.0 + g1) * mix[i]) * lnw_ref[...] + lnb_ref[...]


def _mix_out(yg, gpool, sgr, x2d, mod, mod_spec, grid, row_map, tm, wts, l, alpha, yg_transposed=False, nsub=1):
    row_spec = pl.BlockSpec((tm, D_MODEL), row_map)
    yg_spec = _const_spec(yg.shape) if yg_transposed else row_spec
    return pl.pallas_call(
        functools.partial(_mix_out_kernel, alpha=alpha, yg_transposed=yg_transposed, nsub=nsub),
        grid=grid,
        in_specs=[yg_spec] + [row_spec] * 3
        + [mod_spec, _layer_spec((D_RWKV, D_MODEL), l), _layer_spec((D_MODEL, D_MODEL), l),
           _layer_spec((1, D_MODEL), l), _layer_spec((1, D_MODEL), l)],
        out_specs=row_spec,
        out_shape=jax.ShapeDtypeStruct(x2d.shape, F32),
        compiler_params=_params(("arbitrary",) * len(grid)),
    )(yg, gpool, sgr, x2d, mod, *wts)


def _ffn_prompt_kernel(x_ref, mod_ref, up_ref, cw_ref, cb_ref, down_ref, lnw_ref, lnb_ref,
                       o_ref, conv_ref, carry_ref, *, sub, nsub, alpha):
    @pl.when(pl.program_id(1) == 0)
    def _():
        carry_ref[...] = jnp.zeros(carry_ref.shape, F32)

    rows = [slice(i * sub, (i + 1) * sub) for i in range(nsub)]
    row = lax.broadcasted_iota(jnp.int32, (sub, 1), 0)
    hb = []
    for rs in rows:
        h2 = _ln(x_ref[rs, :]) * (1.0 + mod_ref[:, 4 * D_MODEL:5 * D_MODEL]) + mod_ref[:, 3 * D_MODEL:4 * D_MODEL]
        hb.append(h2.astype(BF16))

    def conv(i, c0, c1):
        u = jnp.dot(hb[i], up_ref[:, c0:c1], preferred_element_type=F32)
        c6 = carry_ref[6:7, c0:c1]
        c7 = carry_ref[7:8, c0:c1]
        p1 = jnp.where(row == 0, c7, pltpu.roll(u, 1, 0))
        p2 = jnp.where(row == 0, c6, jnp.where(row == 1, c7, pltpu.roll(u, 2, 0)))
        carry_ref[6:8, c0:c1] = u[sub - 2:sub, :]
        if i == nsub - 1:
            conv_ref[:, c0:c1] = u[sub - 2:sub, :]
        return cb_ref[:, c0:c1] + p2 * cw_ref[0:1, c0:c1] + p1 * cw_ref[1:2, c0:c1] + u * cw_ref[2:3, c0:c1]

    nchunk = D_FF // FF_CHUNK
    act = []
    for i in range(nsub):
        val = [conv(i, c * FF_CHUNK, (c + 1) * FF_CHUNK) for c in range(nchunk)]
        gate = [conv(i, D_FF + c * FF_CHUNK, D_FF + (c + 1) * FF_CHUNK) for c in range(nchunk)]
        act.append([(val[c] * jax.nn.silu(gate[c])).astype(BF16) for c in range(nchunk)])
    ff = []
    for i in range(nsub):
        acc = jnp.zeros((sub, D_MODEL), F32)
        for c in range(nchunk):
            acc = acc + jnp.dot(act[i][c], down_ref[c * FF_CHUNK:(c + 1) * FF_CHUNK, :], preferred_element_type=F32)
        ff.append(acc)
    g2 = mod_ref[:, 5 * D_MODEL:6 * D_MODEL]
    for i, rs in enumerate(rows):
        o_ref[rs, :] = _ln(alpha * x_ref[rs, :] + (1.0 + g2) * ff[i]) * lnw_ref[...] + lnb_ref[...]


def _ffn_weight_specs(l):
    return [_layer_spec((D_MODEL, 2 * D_FF), l), _layer_spec((CONV_W, 2 * D_FF), l), _layer_spec((1, 2 * D_FF), l),
            _layer_spec((D_FF, D_MODEL), l), _layer_spec((1, D_MODEL), l), _layer_spec((1, D_MODEL), l)]


def _ffn_prompt(x2d, mod3, wts, l, batch, seq, alpha):
    sub = min(ROW_TILE, seq)
    nsub = math.gcd(seq // sub, FFN_SUBTILES)
    tm = sub * nsub
    nj = seq // tm
    row_spec = pl.BlockSpec((tm, D_MODEL), lambda b, j: (b * nj + j, 0))
    return pl.pallas_call(
        functools.partial(_ffn_prompt_kernel, sub=sub, nsub=nsub, alpha=alpha),
        grid=(batch, nj),
        in_specs=[row_spec, pl.BlockSpec((None, 1, 6 * D_MODEL), lambda b, j: (b, 0, 0))] + _ffn_weight_specs(l),
        out_specs=[row_spec, pl.BlockSpec((None, CONV_W - 1, 2 * D_FF), lambda b, j: (b, 0, 0))],
        out_shape=[jax.ShapeDtypeStruct(x2d.shape, F32),
                   jax.ShapeDtypeStruct((batch, CONV_W - 1, 2 * D_FF), F32)],
        scratch_shapes=[pltpu.VMEM((8, 2 * D_FF), F32)],
        compiler_params=_params(("arbitrary", "arbitrary")),
    )(x2d, mod3, *wts)


def _ffn_sample_kernel(x_ref, mod_ref, cst_ref, up_ref, cw_ref, cb_ref, down_ref, lnw_ref, lnb_ref,
                       o_ref, conv_ref, *, alpha):
    x = x_ref[...]
    h2 = _ln(x) * (1.0 + mod_ref[:, 4 * D_MODEL:5 * D_MODEL]) + mod_ref[:, 3 * D_MODEL:4 * D_MODEL]
    hb = h2.astype(BF16)

    def conv(c0, c1):
        u = jnp.dot(hb, up_ref[:, c0:c1], preferred_element_type=F32)
        p2 = cst_ref[:, 0, c0:c1]
        p1 = cst_ref[:, 1, c0:c1]
        conv_ref[:, 0, c0:c1] = p1
        conv_ref[:, 1, c0:c1] = u
        return cb_ref[:, c0:c1] + p2 * cw_ref[0:1, c0:c1] + p1 * cw_ref[1:2, c0:c1] + u * cw_ref[2:3, c0:c1]

    ff = jnp.zeros(x.shape, F32)
    for c in range(D_FF // FF_CHUNK):
        val = conv(c * FF_CHUNK, (c + 1) * FF_CHUNK)
        gate = conv(D_FF + c * FF_CHUNK, D_FF + (c + 1) * FF_CHUNK)
        ff = ff + _dot(val * jax.nn.silu(gate), down_ref[c * FF_CHUNK:(c + 1) * FF_CHUNK, :])
    g2 = mod_ref[:, 5 * D_MODEL:6 * D_MODEL]
    o_ref[...] = _ln(alpha * x + (1.0 + g2) * ff) * lnw_ref[...] + lnb_ref[...]


def _ffn_sample(x2d, mod2, conv_all, wts, l, alpha):
    n = x2d.shape[0]
    st = (n, CONV_W - 1, 2 * D_FF)
    out_full = lambda shape: pl.BlockSpec(shape, lambda i: (0,) * len(shape))
    return pl.pallas_call(
        functools.partial(_ffn_sample_kernel, alpha=alpha),
        grid=(1,),
        in_specs=[_const_spec((n, D_MODEL)), _const_spec((n, 6 * D_MODEL)), _layer_spec(st, l)]
        + _ffn_weight_specs(l),
        out_specs=[out_full((n, D_MODEL)), out_full(st)],
        out_shape=[jax.ShapeDtypeStruct(x2d.shape, F32), jax.ShapeDtypeStruct(st, F32)],
        compiler_params=_params(("arbitrary",)),
    )(x2d, mod2, conv_all, *wts)


def _stacked_weights(p):
    depth = p['w_in'].shape[0]
    row = lambda a: a.reshape(depth, 1, -1)
    bf = lambda a: a.astype(BF16)
    zeros = lambda r, c: jnp.zeros((depth, r, c), F32)
    lora_up = jnp.concatenate([
        jnp.concatenate([p['decay_up'], zeros(D_DECAY_LORA, 2 * D_RWKV)], axis=2),
        jnp.concatenate([zeros(D_AAA_LORA, D_RWKV), p['aaa_up'], zeros(D_AAA_LORA, D_RWKV)], axis=2),
        jnp.concatenate([zeros(D_GATE_LORA, 2 * D_RWKV), p['gate_up']], axis=2)], axis=1)
    head = jnp.arange(2 * HEAD_SIZE) // HEAD_SIZE
    mpair = (head[:, None] == head[None, :]).astype(BF16)
    mix_in = (bf(p['w_in']), row(p['mu_shift']), bf(lora_up), row(p['decay_w0']), row(p['aaa_a0']),
              row(p['k_k']), row(p['k_a']), mpair, bf(p['pool_w']), row(p['pool_scale']), bf(p['w_branch_pool']))
    wkv = (row(p['r_k']), row(p['gn_w']), row(p['gn_b']))
    col = lambda a: a.reshape(depth, -1, 1)
    wkv_cols = (col(p['r_k']), col(p['gn_w']), col(p['gn_b']))
    mix_out = (bf(p['w_branch_rwkv']), bf(p['w_out']), row(p['ln1_w']), row(p['ln1_b']))
    ffn = (bf(p['ffn_up']), p['conv_w'], row(p['conv_b']), bf(p['ffn_down']), row(p['ln2_w']), row(p['ln2_b']))
    return mix_in, wkv, wkv_cols, mix_out, ffn


def _trunk(x_prompt, x_sample, state_shift, state_wkv, state_pool, state_conv, mod, p):
    depth = p['w_in'].shape[0]
    alpha = float((2 * depth) ** 0.25)
    batch, seq, _ = x_prompt.shape
    nsamp = x_sample.shape[0]
    xp = x_prompt.reshape(batch * seq, D_MODEL).astype(F32)
    xs = x_sample.reshape(nsamp, D_MODEL).astype(F32)
    tm = min(ROW_TILE, seq)
    nj = seq // tm
    mo_sub = math.gcd(nj, MIX_OUT_SUBTILES)
    prompt_mod_spec = pl.BlockSpec((None, 1, 6 * D_MODEL), lambda i: (i // (nj // mo_sub), 0, 0))
    sample_mod_spec = pl.BlockSpec((nsamp, 6 * D_MODEL), lambda i: (0, 0))
    w_mix_in, w_wkv, w_wkv_cols, w_mix_out, w_ffn = _stacked_weights(p)
    shift_all = state_shift.astype(F32)
    wkv_all = jnp.transpose(state_wkv.astype(F32), (0, 2, 3, 4, 1))
    pool_all = jnp.swapaxes(state_pool.astype(F32), 1, 2)
    conv_all = state_conv.astype(F32)
    outs = {k: [] for k in ('shift_p', 'wkv_p', 'pool_p', 'conv_p', 'shift_s', 'wkv_s', 'pool_s', 'conv_s')}
    for l in range(depth):
        mod_p = mod[l, 0:batch].reshape(batch, 1, 6 * D_MODEL)
        mod_s = mod[l, batch:batch + nsamp]

        *ops, gpool, sgr, shift_p, pool_p = _mix_in_prompt(xp, mod_p, w_mix_in, l, batch, seq)
        yg, wkv_p = _wkv_prompt(ops, *w_wkv, w_mix_in[7], l, batch, seq)
        xp = _mix_out(yg, gpool, sgr, xp, mod_p, prompt_mod_spec, (batch * nj // mo_sub,), lambda i: (i, 0),
                      tm * mo_sub, w_mix_out, l, alpha, nsub=mo_sub)
        xp, conv_p = _ffn_prompt(xp, mod_p, w_ffn, l, batch, seq, alpha)
        outs['shift_p'].append(shift_p.reshape(batch, D_MODEL))
        outs['wkv_p'].append(wkv_p)
        outs['pool_p'].append(pool_p[:, 1:, :])
        outs['conv_p'].append(conv_p)

        *ops, gpool, sgr, shift_s, pool_s = _mix_in_sample(xs, mod_s, shift_all, pool_all, w_mix_in, l)
        yg, wkv_s = _wkv_sample(ops, *w_wkv_cols, wkv_all, outs['wkv_s'], l)
        xs = _mix_out(yg, gpool, sgr, xs, mod_s, sample_mod_spec, (1,), lambda i: (0, 0), nsamp, w_mix_out, l,
                      alpha, yg_transposed=True)
        xs, conv_s = _ffn_sample(xs, mod_s, conv_all, w_ffn, l, alpha)
        outs['shift_s'].append(shift_s)
        outs['wkv_s'].append(wkv_s)
        outs['pool_s'].append(pool_s)
        outs['conv_s'].append(conv_s)
    wkv_s_all = jnp.transpose(outs.pop('wkv_s')[-1], (0, 4, 1, 2, 3))
    st = {k: jnp.stack(v) for k, v in outs.items()}
    pool_s_all = jnp.swapaxes(st['pool_s'], 1, 2)
    return (xp.reshape(batch, seq, D_MODEL), xs.reshape(nsamp, 1, D_MODEL), st['shift_p'], st['wkv_p'], st['pool_p'],
            st['conv_p'], st['shift_s'], wkv_s_all, pool_s_all, st['conv_s'])


def kernel(x_prompt, x_sample, state_shift, state_wkv, state_pool, state_conv, c_prompt, c_sample, ada_w, ada_b, w_in, mu_shift, decay_w0, decay_up, aaa_a0, aaa_up, gate_up, k_k, k_a, r_k, gn_w, gn_b, w_branch_rwkv, pool_w, pool_scale, w_branch_pool, w_out, ln1_w, ln1_b, ffn_up, conv_w, conv_b, ffn_down, ln2_w, ln2_b):
    p = {'w_in': w_in, 'mu_shift': mu_shift, 'decay_w0': decay_w0, 'decay_up': decay_up, 'aaa_a0': aaa_a0,
         'aaa_up': aaa_up, 'gate_up': gate_up, 'k_k': k_k, 'k_a': k_a, 'r_k': r_k, 'gn_w': gn_w, 'gn_b': gn_b,
         'w_branch_rwkv': w_branch_rwkv, 'pool_w': pool_w, 'pool_scale': pool_scale,
         'w_branch_pool': w_branch_pool, 'w_out': w_out, 'ln1_w': ln1_w, 'ln1_b': ln1_b, 'ffn_up': ffn_up,
         'conv_w': conv_w, 'conv_b': conv_b, 'ffn_down': ffn_down, 'ln2_w': ln2_w, 'ln2_b': ln2_b}
    c_all = jnp.concatenate([c_prompt, c_sample], axis=0).astype(F32)
    mod = _modulation(c_all, ada_w, ada_b)
    dt = x_prompt.dtype
    out = _trunk(x_prompt, x_sample, state_shift, state_wkv, state_pool, state_conv, mod, p)
    return tuple(o.astype(dt) for o in out)
```
